```python
import math
import jax, jax.numpy as jnp
from jax import lax
import numpy as np

D_MODEL = 1024
BATCH = 8
SEQ = 8192
DEPTH = 4

N_MIXERS = 3
N_RET = (DEPTH + 2) // N_MIXERS
N_GDN = (DEPTH + 1) // N_MIXERS
N_SSD = DEPTH // N_MIXERS
D_FF = 4 * D_MODEL
NORM_EPS = 1e-6
CONV_K = 4

RET_HEADS = 4
RET_DK = D_MODEL // RET_HEADS
RET_DV = 2 * D_MODEL // RET_HEADS
RET_QK = RET_HEADS * RET_DK
RET_V = RET_HEADS * RET_DV
RET_IN = 2 * RET_QK + 2 * RET_V
RET_CHUNK = 128
RET_ROPE_BASE = 10000.0

GDN_QK_HEADS = 8
GDN_V_HEADS = 16
GDN_DK = 128
GDN_DV = 128
GDN_QK = GDN_QK_HEADS * GDN_DK
GDN_V = GDN_V_HEADS * GDN_DV
GDN_CONV_CH = 2 * GDN_QK + GDN_V
GDN_IN = GDN_CONV_CH + GDN_V + 2 * GDN_V_HEADS
GDN_CHUNK = 64

SSD_D_INNER = 2 * D_MODEL
SSD_HEAD_DIM = 64
SSD_HEADS = SSD_D_INNER // SSD_HEAD_DIM
SSD_GROUPS = 8
SSD_STATE = 128
SSD_CONV_CH = SSD_D_INNER + 2 * SSD_GROUPS * SSD_STATE
SSD_IN = SSD_D_INNER + SSD_CONV_CH + SSD_HEADS
SSD_CHUNK = 128

kernel_name = 'hybrid_ret_gdn_ssd_adaln_trunk'


def to_chunks(t, size):
    b, s = t.shape[:2]
    return jnp.moveaxis(t.reshape(b, s // size, size, *t.shape[2:]), 1, 0)


def from_chunks(t):
    n, b, size = t.shape[:3]
    return jnp.moveaxis(t, 0, 1).reshape(b, n * size, *t.shape[3:])


def rms_norm(x, g=None):
    xf = x.astype(jnp.float32)
    y = xf * lax.rsqrt(jnp.mean(xf * xf, axis=-1, keepdims=True) + NORM_EPS)
    if g is not None:
        y = y * g.astype(jnp.float32)
    return y.astype(x.dtype)


def l2_normalize(x):
    return x * lax.rsqrt(jnp.sum(x * x, axis=-1, keepdims=True) + 1e-6)


def causal_dwconv(x, w):
    k = w.shape[0]
    return lax.conv_general_dilated(x, w[:, None, :].astype(x.dtype), window_strides=(1,), padding=[(k - 1, 0)], dimension_numbers=('NWC', 'WIO', 'NWC'), feature_group_count=x.shape[-1])


def rotate_pairs(x, cos, sin):
    x1 = x[..., 0::2]
    x2 = x[..., 1::2]
    return jnp.stack([x1 * cos - x2 * sin, x1 * sin + x2 * cos], axis=-1).reshape(x.shape)


def retention_scan(q, k, v):
    b, s, nh, dk = q.shape
    dv = v.shape[-1]
    c = RET_CHUNK
    lg = jnp.log1p(-jnp.exp2(-5.0 - jnp.arange(nh, dtype=jnp.float32)))
    pos = jnp.arange(c, dtype=jnp.float32)
    rel = pos[:, None] - pos[None, :]
    causal = rel >= 0
    dmask = jnp.where(causal, jnp.exp(jnp.where(causal, rel, 0.0) * lg[:, None, None]), 0.0)
    xi = jnp.exp((pos[:, None] + 1.0) * lg[None, :])
    zeta = jnp.exp((c - 1.0 - pos)[:, None] * lg[None, :])
    decay_c = jnp.exp(c * lg)

    def step(r, inp):
        qc, kc, vc = inp
        sc = jnp.einsum('bthd,bshd->bhts', qc, kc) * dmask
        o = jnp.einsum('bhts,bshv->bthv', sc, vc) + jnp.einsum('bthd,bhdv->bthv', qc, r) * xi[None, :, :, None]
        r = r * decay_c[None, :, None, None] + jnp.einsum('bshd,bshv->bhdv', kc * zeta[None, :, :, None], vc)
        return r, o

    r0 = jnp.zeros((b, nh, dk, dv), jnp.float32)
    _, o = lax.scan(step, r0, (to_chunks(q, c), to_chunks(k, c), to_chunks(v, c)))
    return from_chunks(o)


def retention(h, w_in, w_out):
    b, s, _ = h.shape
    q, k, v, g = jnp.split(h @ w_in, [RET_QK, 2 * RET_QK, 2 * RET_QK + RET_V], axis=-1)
    q = q.reshape(b, s, RET_HEADS, RET_DK).astype(jnp.float32)
    k = k.reshape(b, s, RET_HEADS, RET_DK).astype(jnp.float32)
    v = v.reshape(b, s, RET_HEADS, RET_DV).astype(jnp.float32)
    inv_freq = RET_ROPE_BASE ** (-jnp.linspace(0.0, 1.0, RET_DK // 2, dtype=jnp.float32))
    ang = jnp.arange(s, dtype=jnp.float32)[:, None] * inv_freq[None, :]
    cos = jnp.cos(ang)[None, :, None, :]
    sin = jnp.sin(ang)[None, :, None, :]
    q = rotate_pairs(q, cos, sin)
    k = rotate_pairs(k, cos, sin) * (RET_DK ** -0.5)
    o = retention_scan(q, k, v)
    o = rms_norm(o).reshape(b, s, RET_V).astype(h.dtype)
    return (jax.nn.silu(g) * o) @ w_out


def gated_delta_scan(q, k, v, beta, g):
    b, s, nh, dk = q.shape
    dv = v.shape[-1]
    c = GDN_CHUNK
    incl = jnp.tril(jnp.ones((c, c), bool))
    strict = jnp.tril(jnp.ones((c, c), bool), -1)
    eye = jnp.eye(c, dtype=jnp.float32)

    def step(st, inp):
        qc, kc, vc, bc, gc = inp
        gcum = jnp.cumsum(gc, axis=1)
        gh = jnp.swapaxes(gcum, 1, 2)
        rel = gh[..., :, None] - gh[..., None, :]
        decay = jnp.where(incl, jnp.exp(jnp.where(incl, rel, 0.0)), 0.0)
        kb = kc * bc[..., None]
        a_mat = jnp.einsum('bthd,bshd->bhts', kb, kc) * decay
        lmat = eye + jnp.where(strict, a_mat, 0.0)
        rhs = jnp.concatenate([vc * bc[..., None], kb * jnp.exp(gcum)[..., None]], axis=-1)
        sol = lax.linalg.triangular_solve(lmat, jnp.swapaxes(rhs, 1, 2), left_side=True, lower=True, unit_diagonal=True)
        u, w = sol[..., :dv], sol[..., dv:]
        v_new = u - jnp.einsum('bhtd,bhdv->bhtv', w, st)
        qk = jnp.einsum('bthd,bshd->bhts', qc, kc) * decay
        q_dec = jnp.swapaxes(qc * jnp.exp(gcum)[..., None], 1, 2)
        o = jnp.einsum('bhtd,bhdv->bhtv', q_dec, st) + jnp.einsum('bhts,bhsv->bhtv', qk, v_new)
        g_last = gh[..., -1]
        k_dec = kc * jnp.exp(g_last[:, None, :] - gcum)[..., None]
        st = st * jnp.exp(g_last)[..., None, None] + jnp.einsum('bshd,bhsv->bhdv', k_dec, v_new)
        return st, jnp.swapaxes(o, 1, 2)

    st0 = jnp.zeros((b, nh, dk, dv), jnp.float32)
    _, o = lax.scan(step, st0, (to_chunks(q, c), to_chunks(k, c), to_chunks(v, c), to_chunks(beta, c), to_chunks(g, c)))
    return from_chunks(o)


def gated_deltanet(h, w_in, conv_w, a_log, dt_bias, norm_g, w_out):
    b, s, _ = h.shape
    qkv, z, beta_in, a_in = jnp.split(h @ w_in, [GDN_CONV_CH, GDN_CONV_CH + GDN_V, GDN_CONV_CH + GDN_V + GDN_V_HEADS], axis=-1)
    qkv = jax.nn.silu(causal_dwconv(qkv, conv_w))
    q, k, v = jnp.split(qkv, [GDN_QK, 2 * GDN_QK], axis=-1)
    rep = GDN_V_HEADS // GDN_QK_HEADS
    q = jnp.repeat(q.reshape(b, s, GDN_QK_HEADS, GDN_DK), rep, axis=2).astype(jnp.float32)
    k = jnp.repeat(k.reshape(b, s, GDN_QK_HEADS, GDN_DK), rep, axis=2).astype(jnp.float32)
    v = v.reshape(b, s, GDN_V_HEADS, GDN_DV).astype(jnp.float32)
    q = l2_normalize(q) * (GDN_DK ** -0.5)
    k = l2_normalize(k)
    beta = jax.nn.sigmoid(beta_in.astype(jnp.float32))
    g = -jnp.exp(a_log.astype(jnp.float32)) * jax.nn.softplus(a_in.astype(jnp.float32) + dt_bias.astype(jnp.float32))
    o = gated_delta_scan(q, k, v, beta, g)
    o = rms_norm(o, norm_g) * jax.nn.silu(z.reshape(b, s, GDN_V_HEADS, GDN_DV).astype(jnp.float32))
    return o.reshape(b, s, GDN_V).astype(h.dtype) @ w_out


def ssd_scan(x, dt, a, bm, cm):
    b, s, nh, p = x.shape
    ng, n = bm.shape[2:]
    j = nh // ng
    c = SSD_CHUNK
    incl = jnp.tril(jnp.ones((c, c), bool))

    def step(st, inp):
        xc, dtc, bc, cc = inp
        acum = jnp.cumsum(dtc * a, axis=1)
        ah = jnp.moveaxis(acum, 1, 2).reshape(b, ng, j, c)
        rel = ah[..., :, None] - ah[..., None, :]
        seg = jnp.where(incl, jnp.exp(jnp.where(incl, rel, 0.0)), 0.0)
        cb = jnp.einsum('btgn,bsgn->bgts', cc, bc)
        xdt = (xc * dtc[..., None]).reshape(b, c, ng, j, p)
        y_diag = jnp.einsum('bgjts,bsgjp->btgjp', cb[:, :, None] * seg, xdt)
        y_off = jnp.einsum('btgn,bgjpn->btgjp', cc, st) * jnp.exp(acum).reshape(b, c, ng, j)[..., None]
        a_last = ah[..., -1]
        w_dec = jnp.exp(a_last[..., None] - ah)
        st = st * jnp.exp(a_last)[..., None, None] + jnp.einsum('bsgn,bgjs,bsgjp->bgjpn', bc, w_dec, xdt)
        return st, (y_diag + y_off).reshape(b, c, nh, p)

    st0 = jnp.zeros((b, ng, j, p, n), jnp.float32)
    _, y = lax.scan(step, st0, (to_chunks(x, c), to_chunks(dt, c), to_chunks(bm, c), to_chunks(cm, c)))
    return from_chunks(y)


def mamba2_ssd(h, w_in, conv_w, conv_b, a_log, dt_bias, d_skip, norm_g, w_out):
    b, s, _ = h.shape
    z, xbc, dt = jnp.split(h @ w_in, [SSD_D_INNER, SSD_D_INNER + SSD_CONV_CH], axis=-1)
    xbc = jax.nn.silu(causal_dwconv(xbc, conv_w) + conv_b)
    xs, bm, cm = jnp.split(xbc, [SSD_D_INNER, SSD_D_INNER + SSD_GROUPS * SSD_STATE], axis=-1)
    xs = xs.reshape(b, s, SSD_HEADS, SSD_HEAD_DIM).astype(jnp.float32)
    bm = bm.reshape(b, s, SSD_GROUPS, SSD_STATE).astype(jnp.float32)
    cm = cm.reshape(b, s, SSD_GROUPS, SSD_STATE).astype(jnp.float32)
    dt = jax.nn.softplus(dt.astype(jnp.float32) + dt_bias.astype(jnp.float32))
    a = -jnp.exp(a_log.astype(jnp.float32))
    y = ssd_scan(xs, dt, a, bm, cm) + d_skip.astype(jnp.float32)[:, None] * xs
    y = y.reshape(b, s, SSD_D_INNER) * jax.nn.silu(z.astype(jnp.float32))
    y = rms_norm(y.reshape(b, s, SSD_GROUPS, SSD_D_INNER // SSD_GROUPS)).reshape(b, s, SSD_D_INNER) * norm_g.astype(jnp.float32)
    return y.astype(h.dtype) @ w_out


def squared_relu_mlp(h, w1, w2):
    return jnp.square(jax.nn.relu(h @ w1)) @ w2


def _dt_bias(key, shape):
    dt = jnp.exp(jax.random.uniform(key, shape, jnp.float32, math.log(1e-3), math.log(1e-1)))
    return dt + jnp.log(-jnp.expm1(-dt))


def setup_inputs(seed: int = 0) -> dict:
    key = jax.random.key(seed)
    ks = jax.random.split(key, 26)

    def nrm(i, shape, scale):
        return jax.random.normal(ks[i], shape, jnp.float32) * scale

    return {
        'x': nrm(0, (BATCH, SEQ, D_MODEL), 1.0),
        'c': nrm(1, (BATCH, D_MODEL), 1.0),
        'ada_w': nrm(2, (DEPTH, D_MODEL, 6 * D_MODEL), 0.1 * D_MODEL ** -0.5),
        'ada_b': nrm(3, (DEPTH, 6 * D_MODEL), 0.01),
        'norm_mix_g': 1.0 + nrm(4, (DEPTH, D_MODEL), 0.02),
        'norm_mlp_g': 1.0 + nrm(5, (DEPTH, D_MODEL), 0.02),
        'mlp_w1': nrm(6, (DEPTH, D_MODEL, D_FF), D_MODEL ** -0.5),
        'mlp_w2': nrm(7, (DEPTH, D_FF, D_MODEL), D_FF ** -0.5),
        'final_norm_g': 1.0 + nrm(8, (D_MODEL,), 0.02),
        'ret_w_in': nrm(9, (N_RET, D_MODEL, RET_IN), D_MODEL ** -0.5),
        'ret_w_out': nrm(10, (N_RET, RET_V, D_MODEL), RET_V ** -0.5),
        'gdn_w_in': nrm(11, (N_GDN, D_MODEL, GDN_IN), D_MODEL ** -0.5),
        'gdn_conv_w': nrm(12, (N_GDN, CONV_K, GDN_CONV_CH), CONV_K ** -0.5),
        'gdn_A_log': jnp.log(jax.random.uniform(ks[13], (N_GDN, GDN_V_HEADS), jnp.float32, 1.0, 16.0)),
        'gdn_dt_bias': _dt_bias(ks[14], (N_GDN, GDN_V_HEADS)),
        'gdn_norm_g': 1.0 + nrm(15, (N_GDN, GDN_DV), 0.02),
        'gdn_w_out': nrm(16, (N_GDN, GDN_V, D_MODEL), GDN_V ** -0.5),
        'ssd_w_in': nrm(17, (N_SSD, D_MODEL, SSD_IN), D_MODEL ** -0.5),
        'ssd_conv_w': nrm(18, (N_SSD, CONV_K, SSD_CONV_CH), CONV_K ** -0.5),
        'ssd_conv_b': nrm(19, (N_SSD, SSD_CONV_CH), 0.01),
        'ssd_A_log': jnp.log(jax.random.uniform(ks[20], (N_SSD, SSD_HEADS), jnp.float32, 1.0, 16.0)),
        'ssd_dt_bias': _dt_bias(ks[21], (N_SSD, SSD_HEADS)),
        'ssd_D': 1.0 + nrm(22, (N_SSD, SSD_HEADS), 0.1),
        'ssd_norm_g': 1.0 + nrm(23, (N_SSD, SSD_D_INNER), 0.02),
        'ssd_w_out': nrm(24, (N_SSD, SSD_D_INNER, D_MODEL), SSD_D_INNER ** -0.5),
    }


def reference(x, c, ada_w, ada_b, norm_mix_g, norm_mlp_g, mlp_w1, mlp_w2, final_norm_g, ret_w_in, ret_w_out, gdn_w_in, gdn_conv_w, gdn_A_log, gdn_dt_bias, gdn_norm_g, gdn_w_out, ssd_w_in, ssd_conv_w, ssd_conv_b, ssd_A_log, ssd_dt_bias, ssd_D, ssd_norm_g, ssd_w_out):
    cond = jax.nn.silu(c)
    for l in range(DEPTH):
        mod = cond @ ada_w[l] + ada_b[l]
        sh1, sc1, gt1, sh2, sc2, gt2 = [m[:, None, :] for m in jnp.split(mod, 6, axis=-1)]
        h = rms_norm(x, norm_mix_g[l]) * (1.0 + sc1) + sh1
        kind = l % N_MIXERS
        j = l // N_MIXERS
        if kind == 0:
            y = retention(h, ret_w_in[j], ret_w_out[j])
        elif kind == 1:
            y = gated_deltanet(h, gdn_w_in[j], gdn_conv_w[j], gdn_A_log[j], gdn_dt_bias[j], gdn_norm_g[j], gdn_w_out[j])
        else:
            y = mamba2_ssd(h, ssd_w_in[j], ssd_conv_w[j], ssd_conv_b[j], ssd_A_log[j], ssd_dt_bias[j], ssd_D[j], ssd_norm_g[j], ssd_w_out[j])
        x = x + (1.0 + gt1) * y
        h = rms_norm(x, norm_mlp_g[l]) * (1.0 + sc2) + sh2
        x = x + (1.0 + gt2) * squared_relu_mlp(h, mlp_w1[l], mlp_w2[l])
    return rms_norm(x, final_norm_g)
```

```python
import functools
import math

import numpy as np
import jax
import jax.numpy as jnp
from jax import lax
from jax.experimental import pallas as pl
from jax.experimental.pallas import tpu as pltpu

F32 = jnp.float32
BF16 = jnp.bfloat16

NORM_EPS = 1e-6
CONV_K = 4
N_MIXERS = 3

RET_HEADS = 4
RET_ROPE_BASE = 10000.0
RET_CHUNK = 256

GDN_QK_HEADS = 8
GDN_V_HEADS = 16
GDN_DK = 128
GDN_DV = 128
GDN_CHUNK = 64

SSD_HEAD_DIM = 64
SSD_GROUPS = 8
SSD_STATE = 128
SSD_CHUNK = 128

LANES = 128
SUBLANES = 8
VMEM_LIMIT_BYTES = 56 * 1024 * 1024

NT_DIMS = (((1,), (1,)), ((), ()))
TN_DIMS = (((0,), (0,)), ((), ()))


def _dot(a, b):
    return jnp.dot(a.astype(BF16), b.astype(BF16), preferred_element_type=F32)


def _dot_nt(a, b):
    return lax.dot_general(a.astype(BF16), b.astype(BF16), NT_DIMS, preferred_element_type=F32)


def _dot_tn(a, b):
    return lax.dot_general(a.astype(BF16), b.astype(BF16), TN_DIMS, preferred_element_type=F32)


def _silu(x):
    return x * jax.nn.sigmoid(x)


def _softplus(x):
    return jnp.maximum(x, 0.0) + jnp.log1p(jnp.exp(-jnp.abs(x)))


def _params(sem):
    return pltpu.CompilerParams(dimension_semantics=sem, vmem_limit_bytes=VMEM_LIMIT_BYTES)


def _const_spec(shape):
    nd = len(shape)
    return pl.BlockSpec(shape, lambda *_: (0,) * nd, pipeline_mode=pl.Buffered(1))


def _mod_kernel(c_ref, w_ref, b_ref, o_ref):
    cond = _silu(c_ref[...])
    o_ref[0] = jnp.dot(cond, w_ref[0], precision=lax.Precision.HIGHEST,
                       preferred_element_type=F32) + b_ref[0]


def _modulation(c, ada_w, ada_b):
    depth, d, d6 = ada_w.shape
    b = c.shape[0]
    return pl.pallas_call(
        _mod_kernel,
        grid=(depth, d6 // d),
        in_specs=[pl.BlockSpec((b, d), lambda l, j: (0, 0)),
                  pl.BlockSpec((1, d, d), lambda l, j: (l, 0, j)),
                  pl.BlockSpec((1, 1, d), lambda l, j: (l, 0, j))],
        out_specs=pl.BlockSpec((1, b, d), lambda l, j: (l, 0, j)),
        out_shape=jax.ShapeDtypeStruct((depth, b, d6), F32),
        compiler_params=_params(("arbitrary", "arbitrary")),
        name="adaln_mod",
    )(c, ada_w, ada_b.reshape(depth, 1, d6))


def _norm_mod(x, gamma, scale, shift):
    ms = jnp.mean(x * x, axis=-1, keepdims=True)
    return x * lax.rsqrt(ms + NORM_EPS) * gamma * (1.0 + scale) + shift


def _inproj_kernel(x_ref, sh_ref, sc_ref, g_ref, w_ref, *rest, n_chunk, has_gate):
    if has_gate:
        wg_ref, o_ref, og_ref = rest
    else:
        (o_ref,) = rest
    h = _norm_mod(x_ref[0], g_ref[...], sc_ref[0], sh_ref[0]).astype(BF16)
    n = w_ref.shape[1]
    for j in range(n // n_chunk):
        sl = slice(j * n_chunk, (j + 1) * n_chunk)
        o_ref[0, :, sl] = jnp.dot(h, w_ref[:, sl], preferred_element_type=F32).astype(BF16)
    if has_gate:
        og_ref[0] = jnp.dot(h, wg_ref[...], preferred_element_type=F32)


def _inproj(x, shift, scale, gamma, w, wg=None, tm=512, n_chunk=512):
    b, s, d = x.shape
    n = w.shape[1]
    tm = min(tm, s)
    has_gate = wg is not None
    row = pl.BlockSpec((1, 1, d), lambda i, t: (i, 0, 0))
    in_specs = [pl.BlockSpec((1, tm, d), lambda i, t: (i, t, 0)), row, row,
                _const_spec((1, d)), _const_spec((d, n))]
    args = [x, shift, scale, gamma.reshape(1, d), w]
    out_specs = [pl.BlockSpec((1, tm, n), lambda i, t: (i, t, 0))]
    out_shape = [jax.ShapeDtypeStruct((b, s, n), BF16)]
    if has_gate:
        in_specs.append(_const_spec((d, LANES)))
        args.append(wg)
        out_specs.append(pl.BlockSpec((1, tm, LANES), lambda i, t: (i, t, 0)))
        out_shape.append(jax.ShapeDtypeStruct((b, s, LANES), F32))
    out = pl.pallas_call(
        functools.partial(_inproj_kernel, n_chunk=n_chunk, has_gate=has_gate),
        grid=(b, s // tm),
        in_specs=in_specs, out_specs=out_specs, out_shape=out_shape,
        compiler_params=_params(("arbitrary", "arbitrary")),
        name="norm_inproj",
    )(*args)
    return out if has_gate else (out[0], None)


def _outmlp_kernel(o_ref, x_ref, gt1_ref, sh2_ref, sc2_ref, gt2_ref, g2_ref, wo_ref, w1_ref,
                   w2_ref, *rest, ff_chunk, final):
    if final:
        gf_ref, y_ref = rest
    else:
        (y_ref,) = rest
    y = jnp.dot(o_ref[0], wo_ref[...], preferred_element_type=F32)
    x1 = x_ref[0] + (1.0 + gt1_ref[0]) * y
    h2 = _norm_mod(x1, g2_ref[...], sc2_ref[0], sh2_ref[0]).astype(BF16)
    d_ff = w1_ref.shape[1]
    m = jnp.zeros_like(x1)
    for j in range(d_ff // ff_chunk):
        sl = slice(j * ff_chunk, (j + 1) * ff_chunk)
        u = jnp.maximum(jnp.dot(h2, w1_ref[:, sl], preferred_element_type=F32), 0.0)
        m = m + jnp.dot((u * u).astype(BF16), w2_ref[sl, :], preferred_element_type=F32)
    x2 = x1 + (1.0 + gt2_ref[0]) * m
    if final:
        ms = jnp.mean(x2 * x2, axis=-1, keepdims=True)
        x2 = x2 * lax.rsqrt(ms + NORM_EPS) * gf_ref[...]
    y_ref[0] = x2


def _outmlp(o, x, gt1, sh2, sc2, gt2, g2, w_out, w1, w2, gf=None, tm=256, ff_chunk=1024):
    b, s, d = x.shape
    dv = o.shape[-1]
    d_ff = w1.shape[1]
    tm = min(tm, s)
    final = gf is not None
    row = pl.BlockSpec((1, 1, d), lambda i, t: (i, 0, 0))
    tile = pl.BlockSpec((1, tm, d), lambda i, t: (i, t, 0))
    in_specs = [pl.BlockSpec((1, tm, dv), lambda i, t: (i, t, 0)), tile, row, row, row, row,
                _const_spec((1, d)), _const_spec((dv, d)), _const_spec((d, d_ff)),
                _const_spec((d_ff, d))]
    args = [o, x, gt1, sh2, sc2, gt2, g2.reshape(1, d), w_out, w1, w2]
    if final:
        in_specs.append(_const_spec((1, d)))
        args.append(gf.reshape(1, d))
    return pl.pallas_call(
        functools.partial(_outmlp_kernel, ff_chunk=ff_chunk, final=final),
        grid=(b, s // tm),
        in_specs=in_specs, out_specs=tile,
        out_shape=jax.ShapeDtypeStruct((b, s, d), F32),
        compiler_params=_params(("arbitrary", "arbitrary")),
        name="outproj_mlp",
    )(*args)


def _causal_conv(x, tail_ref, w):
    c = x.shape[0]
    xe = jnp.concatenate([tail_ref[...], x], axis=0)
    tail_ref[...] = x[c - SUBLANES:, :]
    acc = x * w[CONV_K - 1:CONV_K, :]
    for k in range(1, CONV_K):
        acc = acc + pltpu.roll(xe, k, 0)[SUBLANES:, :] * w[CONV_K - 1 - k:CONV_K - k, :]
    return acc


def _cumsum_rows(x):
    c = x.shape[0]
    row = lax.broadcasted_iota(jnp.int32, x.shape, 0)
    sh = 1
    while sh < c:
        x = x + jnp.where(row >= sh, pltpu.roll(x, sh, 0), 0.0)
        sh *= 2
    return x


def _tri_masks(c):
    t = lax.broadcasted_iota(jnp.int32, (c, c), 0)
    s = lax.broadcasted_iota(jnp.int32, (c, c), 1)
    return t >= s, t > s


def _decay_matrix(col, rowv, incl):
    return jnp.where(incl, jnp.exp(jnp.where(incl, col - rowv, 0.0)), 0.0)


def _ret_kernel(q_ref, k_ref, v_ref, g_ref, invf_ref, dmask_ref, dvec_ref, o_ref, r_ref, *,
                heads):
    c = q_ref.shape[1]
    dk = q_ref.shape[2] // heads
    dv = v_ref.shape[2] // heads
    half = dk // 2
    step = pl.program_id(1)

    @pl.when(step == 0)
    def _():
        r_ref[...] = jnp.zeros_like(r_ref)

    pos = (step * c + lax.broadcasted_iota(jnp.int32, (c, half), 0)).astype(F32)
    ang = pos * invf_ref[...]
    cs = jnp.cos(ang)
    sn = jnp.sin(ang)
    dvec = dvec_ref[...]

    def rope(ref, h):
        a = ref[0, :, h * dk:h * dk + half].astype(F32)
        b = ref[0, :, h * dk + half:(h + 1) * dk].astype(F32)
        return jnp.concatenate([a * cs - b * sn, a * sn + b * cs], axis=1)

    for h in range(heads):
        qh = rope(q_ref, h)
        kh = rope(k_ref, h) * (dk ** -0.5)
        vh = v_ref[0, :, h * dv:(h + 1) * dv]
        xi = dvec[:, h:h + 1]
        zeta = dvec[:, heads + h:heads + h + 1]
        dc = dvec[0:1, 2 * heads + h:2 * heads + h + 1]
        sc = _dot_nt(qh, kh) * dmask_ref[h]
        r = r_ref[h]
        o = _dot(sc, vh) + _dot(qh, r) * xi
        r_ref[h] = r * dc + _dot_tn(kh * zeta, vh)
        ms = jnp.mean(o * o, axis=-1, keepdims=True)
        on = o * lax.rsqrt(ms + NORM_EPS)
        gate = _silu(g_ref[0, :, h * dv:(h + 1) * dv].astype(F32))
        o_ref[0, :, h * dv:(h + 1) * dv] = (gate * on).astype(BF16)


def _retention_scan(qkvg, heads, dk, dv):
    b, s, _ = qkvg.shape
    c = min(RET_CHUNK, s)
    nqk = heads * dk
    nv = heads * dv
    lg = jnp.log1p(-jnp.exp2(-5.0 - jnp.arange(heads, dtype=F32)))
    pos = jnp.arange(c, dtype=F32)
    rel = pos[:, None] - pos[None, :]
    causal = rel >= 0
    dmask = jnp.where(causal, jnp.exp(jnp.where(causal, rel, 0.0) * lg[:, None, None]), 0.0)
    xi = jnp.exp((pos[:, None] + 1.0) * lg[None, :])
    zeta = jnp.exp((c - 1.0 - pos)[:, None] * lg[None, :])
    dcs = jnp.broadcast_to(jnp.exp(c * lg)[None, :], (c, heads))
    dvec = jnp.concatenate([xi, zeta, dcs, jnp.zeros((c, LANES - 3 * heads), F32)], axis=1)
    inv_freq = RET_ROPE_BASE ** (-jnp.linspace(0.0, 1.0, dk // 2, dtype=F32))
    assert nv == 2 * nqk
    return pl.pallas_call(
        functools.partial(_ret_kernel, heads=heads),
        grid=(b, s // c),
        in_specs=[pl.BlockSpec((1, c, nqk), lambda i, t: (i, t, 0)),
                  pl.BlockSpec((1, c, nqk), lambda i, t: (i, t, 1)),
                  pl.BlockSpec((1, c, nv), lambda i, t: (i, t, 1)),
                  pl.BlockSpec((1, c, nv), lambda i, t: (i, t, 2)),
                  _const_spec((1, dk // 2)), _const_spec((heads, c, c)), _const_spec((c, LANES))],
        out_specs=pl.BlockSpec((1, c, nv), lambda i, t: (i, t, 0)),
        out_shape=jax.ShapeDtypeStruct((b, s, nv), BF16),
        scratch_shapes=[pltpu.VMEM((heads, dk, dv), F32)],
        compiler_params=_params(("arbitrary", "arbitrary")),
        name="retention_scan",
    )(qkvg, qkvg, qkvg, qkvg, inv_freq.reshape(1, dk // 2), dmask, dvec)


def _gdn_kernel(qkv_ref, z_ref, gate_ref, cw_ref, par_ref, ng_ref, o_ref, tail_ref, st_ref):
    c = qkv_ref.shape[1]
    nqk = GDN_QK_HEADS * GDN_DK
    rep = GDN_V_HEADS // GDN_QK_HEADS
    step = pl.program_id(1)

    @pl.when(step == 0)
    def _():
        tail_ref[...] = jnp.zeros_like(tail_ref)
        st_ref[...] = jnp.zeros_like(st_ref)

    qkv = _silu(_causal_conv(qkv_ref[0].astype(F32), tail_ref, cw_ref[...]))

    gates = gate_ref[0]
    beta = jax.nn.sigmoid(gates)
    g = -jnp.exp(par_ref[0:1, :]) * _softplus(gates + par_ref[1:2, :])
    gc = _cumsum_rows(g)
    gct = jnp.concatenate([gc, jnp.zeros((LANES - c, LANES), F32)], axis=0).T
    eg = jnp.exp(gc)
    glast = gc[c - 1:c, :]
    eglast = jnp.exp(glast)
    kdec = jnp.exp(glast - gc)
    incl, strict = _tri_masks(c)
    ng = ng_ref[...]

    for j in range(GDN_QK_HEADS):
        q = qkv[:, j * GDN_DK:(j + 1) * GDN_DK]
        k = qkv[:, nqk + j * GDN_DK:nqk + (j + 1) * GDN_DK]
        q = q * lax.rsqrt(jnp.sum(q * q, axis=-1, keepdims=True) + 1e-6) * (GDN_DK ** -0.5)
        k = k * lax.rsqrt(jnp.sum(k * k, axis=-1, keepdims=True) + 1e-6)
        kk = _dot_nt(k, k)
        qk = _dot_nt(q, k)
        for i in range(rep):
            h = j * rep + i
            v = qkv[:, 2 * nqk + h * GDN_DV:2 * nqk + (h + 1) * GDN_DV]
            bcol = beta[:, h:h + 1]
            gcol = gc[:, GDN_V_HEADS + h:GDN_V_HEADS + h + 1]
            grow = gct[GDN_V_HEADS + h:GDN_V_HEADS + h + 1, :c]
            decay = _decay_matrix(gcol, grow, incl)
            p = jnp.where(strict, -(bcol * kk * decay), 0.0)
            x = jnp.concatenate([v * bcol, k * (bcol * eg[:, GDN_V_HEADS + h:GDN_V_HEADS + h + 1])],
                                axis=1)
            n_steps = int(math.log2(c))
            for it in range(n_steps):
                x = x + _dot(p, x)
                if it + 1 < n_steps:
                    p = _dot(p, p)
            u = x[:, :GDN_DV]
            w = x[:, GDN_DV:]
            st = st_ref[h]
            v_new = u - _dot(w, st)
            o = _dot(q * eg[:, GDN_V_HEADS + h:GDN_V_HEADS + h + 1], st) + _dot(qk * decay, v_new)
            st_ref[h] = st * eglast[:, GDN_V_HEADS + h:GDN_V_HEADS + h + 1] + _dot_tn(
                k * kdec[:, GDN_V_HEADS + h:GDN_V_HEADS + h + 1], v_new)
            ms = jnp.mean(o * o, axis=-1, keepdims=True)
            on = o * lax.rsqrt(ms + NORM_EPS) * ng
            zg = _silu(z_ref[0, :, h * GDN_DV:(h + 1) * GDN_DV].astype(F32))
            o_ref[0, :, h * GDN_DV:(h + 1) * GDN_DV] = (on * zg).astype(BF16)


def _gdn_scan(main, gates, conv_w, a_log, dt_bias, norm_g):
    b, s, _ = main.shape
    c = GDN_CHUNK
    nconv = 2 * GDN_QK_HEADS * GDN_DK + GDN_V_HEADS * GDN_DV
    nv = GDN_V_HEADS * GDN_DV
    assert nconv == 2 * nv
    pad = jnp.zeros((GDN_V_HEADS,), F32)
    tail = jnp.zeros((LANES - 2 * GDN_V_HEADS,), F32)
    par = jnp.stack([jnp.concatenate([pad, a_log, tail]), jnp.concatenate([pad, dt_bias, tail])])
    return pl.pallas_call(
        _gdn_kernel,
        grid=(b, s // c),
        in_specs=[pl.BlockSpec((1, c, nconv), lambda i, t: (i, t, 0)),
                  pl.BlockSpec((1, c, nv), lambda i, t: (i, t, 2)),
                  pl.BlockSpec((1, c, LANES), lambda i, t: (i, t, 0)),
                  _const_spec((CONV_K, nconv)), _const_spec((2, LANES)), _const_spec((1, GDN_DV))],
        out_specs=pl.BlockSpec((1, c, nv), lambda i, t: (i, t, 0)),
        out_shape=jax.ShapeDtypeStruct((b, s, nv), BF16),
        scratch_shapes=[pltpu.VMEM((SUBLANES, nconv), F32),
                        pltpu.VMEM((GDN_V_HEADS, GDN_DK, GDN_DV), F32)],
        compiler_params=_params(("arbitrary", "arbitrary")),
        name="gdn_scan",
    )(main, main, gates, conv_w, par, norm_g.reshape(1, GDN_DV))


def _ssd_kernel(z_ref, x_ref, bc_ref, dt_ref, cw_ref, cb_ref, par_ref, ng_ref, o_ref, tail_ref,
                st_ref, y_ref, *, heads):
    c = x_ref.shape[1]
    d_inner = x_ref.shape[2]
    p_dim = d_inner // heads
    gsz = SSD_STATE
    hpg = heads // SSD_GROUPS
    step = pl.program_id(1)

    @pl.when(step == 0)
    def _():
        tail_ref[...] = jnp.zeros_like(tail_ref)
        st_ref[...] = jnp.zeros_like(st_ref)

    xbc = jnp.concatenate([x_ref[0], bc_ref[0]], axis=1).astype(F32)
    xbc = _silu(_causal_conv(xbc, tail_ref, cw_ref[...]) + cb_ref[...])

    dt = _softplus(dt_ref[0] + par_ref[1:2, :])
    a = -jnp.exp(par_ref[0:1, :])
    acum = _cumsum_rows(dt * a)
    acum_t = jnp.concatenate([acum, jnp.zeros((LANES - c, LANES), F32)], axis=0).T if c < LANES \
        else acum.T
    ea = jnp.exp(acum)
    alast = acum[c - 1:c, :]
    ealast = jnp.exp(alast)
    wdec = jnp.exp(alast - acum)
    dskip = par_ref[2:3, :]
    incl, _ = _tri_masks(c)

    for g in range(SSD_GROUPS):
        bm = xbc[:, d_inner + g * gsz:d_inner + (g + 1) * gsz]
        cm = xbc[:, d_inner + SSD_GROUPS * gsz + g * gsz:d_inner + SSD_GROUPS * gsz + (g + 1) * gsz]
        cb = _dot_nt(cm, bm)
        for j in range(hpg):
            h = g * hpg + j
            xh = xbc[:, h * p_dim:(h + 1) * p_dim]
            seg = _decay_matrix(acum[:, h:h + 1], acum_t[h:h + 1, :c], incl)
            xdt = xh * dt[:, h:h + 1]
            st = st_ref[h]
            y = _dot(cb * seg, xdt) + _dot(cm, st) * ea[:, h:h + 1] + dskip[:, h:h + 1] * xh
            st_ref[h] = st * ealast[:, h:h + 1] + _dot_tn(bm, xdt * wdec[:, h:h + 1])
            y_ref[:, h * p_dim:(h + 1) * p_dim] = y

    gw = d_inner // SSD_GROUPS
    for g in range(SSD_GROUPS):
        sl = slice(g * gw, (g + 1) * gw)
        y = y_ref[:, sl] * _silu(z_ref[0, :, sl].astype(F32))
        ms = jnp.mean(y * y, axis=-1, keepdims=True)
        o_ref[0, :, sl] = (y * lax.rsqrt(ms + NORM_EPS) * ng_ref[:, sl]).astype(BF16)


def _ssd_scan(main, dtraw, conv_w, conv_b, a_log, dt_bias, d_skip, norm_g, heads, d_inner):
    b, s, _ = main.shape
    c = min(SSD_CHUNK, s)
    nbc = 2 * SSD_GROUPS * SSD_STATE
    assert nbc == d_inner
    nconv = d_inner + nbc
    padl = lambda v: jnp.concatenate([v, jnp.zeros((LANES - heads,), F32)])
    par = jnp.stack([padl(a_log), padl(dt_bias), padl(d_skip)])
    tile = lambda j: pl.BlockSpec((1, c, d_inner), lambda i, t: (i, t, j))
    return pl.pallas_call(
        functools.partial(_ssd_kernel, heads=heads),
        grid=(b, s // c),
        in_specs=[tile(0), tile(1), tile(2),
                  pl.BlockSpec((1, c, LANES), lambda i, t: (i, t, 0)),
                  _const_spec((CONV_K, nconv)), _const_spec((1, nconv)), _const_spec((3, LANES)),
                  _const_spec((1, d_inner))],
        out_specs=tile(0),
        out_shape=jax.ShapeDtypeStruct((b, s, d_inner), BF16),
        scratch_shapes=[pltpu.VMEM((SUBLANES, nconv), F32),
                        pltpu.VMEM((heads, SSD_STATE, d_inner // heads), F32),
                        pltpu.VMEM((c, d_inner), F32)],
        compiler_params=_params(("arbitrary", "arbitrary")),
        name="ssd_scan",
    )(main, main, main, dtraw, conv_w, conv_b.reshape(1, nconv), par, norm_g.reshape(1, d_inner))


def _deinterleave_heads(w, heads, dk):
    d = w.shape[0]
    return w.reshape(d, heads, dk // 2, 2).transpose(0, 1, 3, 2).reshape(d, heads * dk)


def _pad_gate_cols(w):
    d, n = w.shape
    return jnp.concatenate([w, jnp.zeros((d, LANES - n), w.dtype)], axis=1)


def kernel(x, c, ada_w, ada_b, norm_mix_g, norm_mlp_g, mlp_w1, mlp_w2, final_norm_g, ret_w_in, ret_w_out, gdn_w_in, gdn_conv_w, gdn_A_log, gdn_dt_bias, gdn_norm_g, gdn_w_out, ssd_w_in, ssd_conv_w, ssd_conv_b, ssd_A_log, ssd_dt_bias, ssd_D, ssd_norm_g, ssd_w_out):
    depth = ada_w.shape[0]
    b, s, d = x.shape
    mod = _modulation(c, ada_w, ada_b)

    for l in range(depth):
        sh1, sc1, gt1, sh2, sc2, gt2 = [mod[l, :, None, i * d:(i + 1) * d] for i in range(6)]
        kind = l % N_MIXERS
        j = l // N_MIXERS
        if kind == 0:
            w_in = ret_w_in[j]
            dk = d // RET_HEADS
            dv = 2 * d // RET_HEADS
            nqk = RET_HEADS * dk
            w_in = jnp.concatenate([_deinterleave_heads(w_in[:, :nqk], RET_HEADS, dk),
                                    _deinterleave_heads(w_in[:, nqk:2 * nqk], RET_HEADS, dk),
                                    w_in[:, 2 * nqk:]], axis=1).astype(BF16)
            proj, _ = _inproj(x, sh1, sc1, norm_mix_g[l], w_in)
            o = _retention_scan(proj, RET_HEADS, dk, dv)
            w_out = ret_w_out[j]
        elif kind == 1:
            w_in = gdn_w_in[j]
            nmain = 2 * GDN_QK_HEADS * GDN_DK + 2 * GDN_V_HEADS * GDN_DV
            proj, gates = _inproj(x, sh1, sc1, norm_mix_g[l], w_in[:, :nmain].astype(BF16),
                                  _pad_gate_cols(w_in[:, nmain:]).astype(BF16))
            o = _gdn_scan(proj, gates, gdn_conv_w[j], gdn_A_log[j], gdn_dt_bias[j], gdn_norm_g[j])
            w_out = gdn_w_out[j]
        else:
            w_in = ssd_w_in[j]
            d_inner = ssd_w_out.shape[1]
            heads = ssd_A_log.shape[1]
            nmain = 2 * d_inner + 2 * SSD_GROUPS * SSD_STATE
            proj, dtraw = _inproj(x, sh1, sc1, norm_mix_g[l], w_in[:, :nmain].astype(BF16),
                                  _pad_gate_cols(w_in[:, nmain:]).astype(BF16))
            o = _ssd_scan(proj, dtraw, ssd_conv_w[j], ssd_conv_b[j], ssd_A_log[j], ssd_dt_bias[j],
                          ssd_D[j], ssd_norm_g[j], heads, d_inner)
            w_out = ssd_w_out[j]
        x = _outmlp(o, x, gt1, sh2, sc2, gt2, norm_mlp_g[l], w_out.astype(BF16),
                    mlp_w1[l].astype(BF16), mlp_w2[l].astype(BF16),
                    gf=final_norm_g if l == depth - 1 else None)
    return x
```

```python
import functools
import math

import numpy as np
import jax
import jax.numpy as jnp
from jax import lax
from jax.experimental import pallas as pl
from jax.experimental.pallas import tpu as pltpu

F32 = jnp.float32
BF16 = jnp.bfloat16

NORM_EPS = 1e-6
CONV_K = 4
N_MIXERS = 3

RET_HEADS = 4
RET_ROPE_BASE = 10000.0
RET_CHUNK = 256

GDN_QK_HEADS = 8
GDN_V_HEADS = 16
GDN_DK = 128
GDN_DV = 128
GDN_CHUNK = 64
GDN_SOLVE_BLOCK = 16

SSD_HEAD_DIM = 64
SSD_GROUPS = 8
SSD_STATE = 128
SSD_CHUNK = 128

LANES = 128
SUBLANES = 8
VMEM_LIMIT_BYTES = 56 * 1024 * 1024

NT_DIMS = (((1,), (1,)), ((), ()))
TN_DIMS = (((0,), (0,)), ((), ()))


def _dot(a, b):
    return jnp.dot(a.astype(BF16), b.astype(BF16), preferred_element_type=F32)


def _dot_nt(a, b):
    return lax.dot_general(a.astype(BF16), b.astype(BF16), NT_DIMS, preferred_element_type=F32)


def _dot_tn(a, b):
    return lax.dot_general(a.astype(BF16), b.astype(BF16), TN_DIMS, preferred_element_type=F32)


def _silu(x):
    return x * jax.nn.sigmoid(x)


def _softplus(x):
    return jnp.maximum(x, 0.0) + jnp.log1p(jnp.exp(-jnp.abs(x)))


def _params(sem):
    return pltpu.CompilerParams(dimension_semantics=sem, vmem_limit_bytes=VMEM_LIMIT_BYTES)


def _const_spec(shape):
    nd = len(shape)
    return pl.BlockSpec(shape, lambda *_: (0,) * nd, pipeline_mode=pl.Buffered(1))


def _mod_kernel(c_ref, w_ref, b_ref, o_ref):
    cond = _silu(c_ref[...])
    o_ref[0] = jnp.dot(cond, w_ref[0], precision=lax.Precision.HIGHEST,
                       preferred_element_type=F32) + b_ref[0]


def _modulation(c, ada_w, ada_b):
    depth, d, d6 = ada_w.shape
    b = c.shape[0]
    return pl.pallas_call(
        _mod_kernel,
        grid=(depth, d6 // d),
        in_specs=[pl.BlockSpec((b, d), lambda l, j: (0, 0)),
                  pl.BlockSpec((1, d, d), lambda l, j: (l, 0, j)),
                  pl.BlockSpec((1, 1, d), lambda l, j: (l, 0, j))],
        out_specs=pl.BlockSpec((1, b, d), lambda l, j: (l, 0, j)),
        out_shape=jax.ShapeDtypeStruct((depth, b, d6), F32),
        compiler_params=_params(("arbitrary", "arbitrary")),
        name="adaln_mod",
    )(c, ada_w, ada_b.reshape(depth, 1, d6))


def _norm_mod(x, gamma, scale, shift):
    ms = jnp.mean(x * x, axis=-1, keepdims=True)
    return x * lax.rsqrt(ms + NORM_EPS) * gamma * (1.0 + scale) + shift


def _inproj_kernel(x_ref, sh_ref, sc_ref, g_ref, w_ref, *rest, n_chunk, has_gate):
    if has_gate:
        wg_ref, o_ref, og_ref = rest
    else:
        (o_ref,) = rest
    h = _norm_mod(x_ref[0], g_ref[...], sc_ref[0], sh_ref[0]).astype(BF16)
    n = w_ref.shape[1]
    for j in range(n // n_chunk):
        sl = slice(j * n_chunk, (j + 1) * n_chunk)
        o_ref[0, :, sl] = jnp.dot(h, w_ref[:, sl], preferred_element_type=F32).astype(BF16)
    if has_gate:
        og_ref[0] = jnp.dot(h, wg_ref[...], preferred_element_type=F32)


def _inproj(x, shift, scale, gamma, w, wg=None, tm=512, n_chunk=512):
    b, s, d = x.shape
    n = w.shape[1]
    tm = min(tm, s)
    has_gate = wg is not None
    row = pl.BlockSpec((1, 1, d), lambda i, t: (i, 0, 0))
    in_specs = [pl.BlockSpec((1, tm, d), lambda i, t: (i, t, 0)), row, row,
                _const_spec((1, d)), _const_spec((d, n))]
    args = [x, shift, scale, gamma.reshape(1, d), w]
    out_specs = [pl.BlockSpec((1, tm, n), lambda i, t: (i, t, 0))]
    out_shape = [jax.ShapeDtypeStruct((b, s, n), BF16)]
    if has_gate:
        in_specs.append(_const_spec((d, LANES)))
        args.append(wg)
        out_specs.append(pl.BlockSpec((1, tm, LANES), lambda i, t: (i, t, 0)))
        out_shape.append(jax.ShapeDtypeStruct((b, s, LANES), F32))
    out = pl.pallas_call(
        functools.partial(_inproj_kernel, n_chunk=n_chunk, has_gate=has_gate),
        grid=(b, s // tm),
        in_specs=in_specs, out_specs=out_specs, out_shape=out_shape,
        compiler_params=_params(("arbitrary", "arbitrary")),
        name="norm_inproj",
    )(*args)
    return out if has_gate else (out[0], None)


def _outmlp_kernel(o_ref, x_ref, gt1_ref, sh2_ref, sc2_ref, gt2_ref, g2_ref, wo_ref, w1_ref,
                   w2_ref, *rest, ff_chunk, final):
    if final:
        gf_ref, y_ref = rest
    else:
        (y_ref,) = rest
    y = jnp.dot(o_ref[0], wo_ref[...], preferred_element_type=F32)
    x1 = x_ref[0] + (1.0 + gt1_ref[0]) * y
    h2 = _norm_mod(x1, g2_ref[...], sc2_ref[0], sh2_ref[0]).astype(BF16)
    d_ff = w1_ref.shape[1]
    m = jnp.zeros_like(x1)
    for j in range(d_ff // ff_chunk):
        sl = slice(j * ff_chunk, (j + 1) * ff_chunk)
        u = jnp.maximum(jnp.dot(h2, w1_ref[:, sl], preferred_element_type=F32), 0.0)
        m = m + jnp.dot((u * u).astype(BF16), w2_ref[sl, :], preferred_element_type=F32)
    x2 = x1 + (1.0 + gt2_ref[0]) * m
    if final:
        ms = jnp.mean(x2 * x2, axis=-1, keepdims=True)
        x2 = x2 * lax.rsqrt(ms + NORM_EPS) * gf_ref[...]
    y_ref[0] = x2


def _outmlp(o, x, gt1, sh2, sc2, gt2, g2, w_out, w1, w2, gf=None, tm=256, ff_chunk=1024):
    b, s, d = x.shape
    dv = o.shape[-1]
    d_ff = w1.shape[1]
    tm = min(tm, s)
    final = gf is not None
    row = pl.BlockSpec((1, 1, d), lambda i, t: (i, 0, 0))
    tile = pl.BlockSpec((1, tm, d), lambda i, t: (i, t, 0))
    in_specs = [pl.BlockSpec((1, tm, dv), lambda i, t: (i, t, 0)), tile, row, row, row, row,
                _const_spec((1, d)), _const_spec((dv, d)), _const_spec((d, d_ff)),
                _const_spec((d_ff, d))]
    args = [o, x, gt1, sh2, sc2, gt2, g2.reshape(1, d), w_out, w1, w2]
    if final:
        in_specs.append(_const_spec((1, d)))
        args.append(gf.reshape(1, d))
    return pl.pallas_call(
        functools.partial(_outmlp_kernel, ff_chunk=ff_chunk, final=final),
        grid=(b, s // tm),
        in_specs=in_specs, out_specs=tile,
        out_shape=jax.ShapeDtypeStruct((b, s, d), F32),
        compiler_params=_params(("arbitrary", "arbitrary")),
        name="outproj_mlp",
    )(*args)


def _causal_conv(x, tail_ref, w):
    c = x.shape[0]
    xe = jnp.concatenate([tail_ref[...], x], axis=0)
    tail_ref[...] = x[c - SUBLANES:, :]
    acc = x * w[CONV_K - 1:CONV_K, :]
    for k in range(1, CONV_K):
        acc = acc + pltpu.roll(xe, k, 0)[SUBLANES:, :] * w[CONV_K - 1 - k:CONV_K - k, :]
    return acc


def _cumsum_rows(x):
    c = x.shape[0]
    row = lax.broadcasted_iota(jnp.int32, x.shape, 0)
    sh = 1
    while sh < c:
        x = x + jnp.where(row >= sh, pltpu.roll(x, sh, 0), 0.0)
        sh *= 2
    return x


def _tri_masks(c):
    t = lax.broadcasted_iota(jnp.int32, (c, c), 0)
    s = lax.broadcasted_iota(jnp.int32, (c, c), 1)
    return t >= s, t > s


def _decay_matrix(col, rowv, incl):
    return jnp.where(incl, jnp.exp(jnp.where(incl, col - rowv, 0.0)), 0.0)


def _ret_kernel(q_ref, k_ref, v_ref, g_ref, invf_ref, dmask_ref, dvec_ref, o_ref, r_ref, *,
                heads):
    c = q_ref.shape[1]
    dk = q_ref.shape[2] // heads
    dv = v_ref.shape[2] // heads
    half = dk // 2
    step = pl.program_id(1)

    @pl.when(step == 0)
    def _():
        r_ref[...] = jnp.zeros_like(r_ref)

    pos = (step * c + lax.broadcasted_iota(jnp.int32, (c, half), 0)).astype(F32)
    ang = pos * invf_ref[...]
    cs = jnp.cos(ang)
    sn = jnp.sin(ang)
    dvec = dvec_ref[...]

    def rope(ref, h):
        a = ref[0, :, h * dk:h * dk + half].astype(F32)
        b = ref[0, :, h * dk + half:(h + 1) * dk].astype(F32)
        return jnp.concatenate([a * cs - b * sn, a * sn + b * cs], axis=1)

    for h in range(heads):
        qh = rope(q_ref, h)
        kh = rope(k_ref, h) * (dk ** -0.5)
        vh = v_ref[0, :, h * dv:(h + 1) * dv]
        xi = dvec[:, h:h + 1]
        zeta = dvec[:, heads + h:heads + h + 1]
        dc = dvec[0:1, 2 * heads + h:2 * heads + h + 1]
        sc = _dot_nt(qh, kh) * dmask_ref[h]
        r = r_ref[h]
        o = _dot(sc, vh) + _dot(qh, r) * xi
        r_ref[h] = r * dc + _dot_tn(kh * zeta, vh)
        ms = jnp.mean(o * o, axis=-1, keepdims=True)
        on = o * lax.rsqrt(ms + NORM_EPS)
        gate = _silu(g_ref[0, :, h * dv:(h + 1) * dv].astype(F32))
        o_ref[0, :, h * dv:(h + 1) * dv] = (gate * on).astype(BF16)


def _retention_scan(qkvg, heads, dk, dv):
    b, s, _ = qkvg.shape
    c = min(RET_CHUNK, s)
    nqk = heads * dk
    nv = heads * dv
    lg = jnp.log1p(-jnp.exp2(-5.0 - jnp.arange(heads, dtype=F32)))
    pos = jnp.arange(c, dtype=F32)
    rel = pos[:, None] - pos[None, :]
    causal = rel >= 0
    dmask = jnp.where(causal, jnp.exp(jnp.where(causal, rel, 0.0) * lg[:, None, None]), 0.0)
    xi = jnp.exp((pos[:, None] + 1.0) * lg[None, :])
    zeta = jnp.exp((c - 1.0 - pos)[:, None] * lg[None, :])
    dcs = jnp.broadcast_to(jnp.exp(c * lg)[None, :], (c, heads))
    dvec = jnp.concatenate([xi, zeta, dcs, jnp.zeros((c, LANES - 3 * heads), F32)], axis=1)
    inv_freq = RET_ROPE_BASE ** (-jnp.linspace(0.0, 1.0, dk // 2, dtype=F32))
    assert nv == 2 * nqk
    return pl.pallas_call(
        functools.partial(_ret_kernel, heads=heads),
        grid=(b, s // c),
        in_specs=[pl.BlockSpec((1, c, nqk), lambda i, t: (i, t, 0)),
                  pl.BlockSpec((1, c, nqk), lambda i, t: (i, t, 1)),
                  pl.BlockSpec((1, c, nv), lambda i, t: (i, t, 1)),
                  pl.BlockSpec((1, c, nv), lambda i, t: (i, t, 2)),
                  _const_spec((1, dk // 2)), _const_spec((heads, c, c)), _const_spec((c, LANES))],
        out_specs=pl.BlockSpec((1, c, nv), lambda i, t: (i, t, 0)),
        out_shape=jax.ShapeDtypeStruct((b, s, nv), BF16),
        scratch_shapes=[pltpu.VMEM((heads, dk, dv), F32)],
        compiler_params=_params(("arbitrary", "arbitrary")),
        name="retention_scan",
    )(qkvg, qkvg, qkvg, qkvg, inv_freq.reshape(1, dk // 2), dmask, dvec)


def _unit_lower_solve(amats, rhs):
    c = amats[0].shape[0]
    blk = GDN_SOLVE_BLOCK
    row = lax.broadcasted_iota(jnp.int32, (c, c), 0)
    colm = lax.broadcasted_iota(jnp.int32, (c, c), 1)
    same = (row // blk) == (colm // blk)
    eye = (row == colm).astype(F32)
    ps = [jnp.where(same, -a, 0.0) for a in amats]
    offs = [jnp.where(same, 0.0, a) for a in amats]
    ts = [eye + p for p in ps]
    ps = [_dot(p, p) for p in ps]
    n_sq = int(math.log2(blk))
    for it in range(1, n_sq):
        if it + 1 < n_sq:
            tps = [_dot(jnp.concatenate([t, p], axis=0), p) for t, p in zip(ts, ps)]
            ts = [t + tp[:c] for t, tp in zip(ts, tps)]
            ps = [tp[c:] for tp in tps]
        else:
            ts = [t + _dot(t, p) for t, p in zip(ts, ps)]
    nps = [_dot(t, off) for t, off in zip(ts, offs)]
    xs = [_dot(t, r) for t, r in zip(ts, rhs)]
    for i in range(1, c // blk):
        lo, hi = i * blk, (i + 1) * blk
        xs = [jnp.concatenate([x[:lo], x[lo:hi] - _dot(n[lo:hi, :], x)] + ([x[hi:]] if hi < c else []),
                              axis=0) for n, x in zip(nps, xs)]
    return xs


def _gdn_kernel(qkv_ref, z_ref, gate_ref, cw_ref, par_ref, ng_ref, o_ref, tail_ref, st_ref):
    c = qkv_ref.shape[1]
    nqk = GDN_QK_HEADS * GDN_DK
    rep = GDN_V_HEADS // GDN_QK_HEADS
    step = pl.program_id(1)

    @pl.when(step == 0)
    def _():
        tail_ref[...] = jnp.zeros_like(tail_ref)
        st_ref[...] = jnp.zeros_like(st_ref)

    qkv = _silu(_causal_conv(qkv_ref[0].astype(F32), tail_ref, cw_ref[...]))

    gates = gate_ref[0]
    beta = jax.nn.sigmoid(gates)
    g = -jnp.exp(par_ref[0:1, :]) * _softplus(gates + par_ref[1:2, :])
    gc = _cumsum_rows(g)
    gct = jnp.concatenate([gc, jnp.zeros((LANES - c, LANES), F32)], axis=0).T
    eg = jnp.exp(gc)
    glast = gc[c - 1:c, :]
    eglast = jnp.exp(glast)
    kdec = jnp.exp(glast - gc)
    incl, strict = _tri_masks(c)
    ng = ng_ref[...]

    heads = range(GDN_V_HEADS)
    col = lambda a, h: a[:, GDN_V_HEADS + h:GDN_V_HEADS + h + 1]
    qs, ks, qkk = [], [], []
    for j in range(GDN_QK_HEADS):
        q = qkv[:, j * GDN_DK:(j + 1) * GDN_DK]
        k = qkv[:, nqk + j * GDN_DK:nqk + (j + 1) * GDN_DK]
        q = q * lax.rsqrt(jnp.sum(q * q, axis=-1, keepdims=True) + 1e-6) * (GDN_DK ** -0.5)
        k = k * lax.rsqrt(jnp.sum(k * k, axis=-1, keepdims=True) + 1e-6)
        qs.append(q)
        ks.append(k)
        qkk.append(_dot_nt(jnp.concatenate([q, k], axis=0), k))

    decays, amats, xs = [], [], []
    for h in heads:
        j = h // rep
        v = qkv[:, 2 * nqk + h * GDN_DV:2 * nqk + (h + 1) * GDN_DV]
        bcol = beta[:, h:h + 1]
        grow = gct[GDN_V_HEADS + h:GDN_V_HEADS + h + 1, :c]
        decay = _decay_matrix(col(gc, h), grow, incl)
        decays.append(decay)
        amats.append(jnp.where(strict, bcol * qkk[j][c:, :] * decay, 0.0))
        xs.append(jnp.concatenate([v * bcol, ks[j] * (bcol * col(eg, h))], axis=1))
    xs = _unit_lower_solve(amats, xs)

    sts = [st_ref[h] for h in heads]
    v_news = [xs[h][:, :GDN_DV] - _dot(xs[h][:, GDN_DV:], sts[h]) for h in heads]
    outs = []
    for h in heads:
        j = h // rep
        lhs = jnp.concatenate([qs[j] * col(eg, h), qkk[j][:c, :] * decays[h]], axis=1)
        outs.append(_dot(lhs, jnp.concatenate([sts[h], v_news[h]], axis=0)))
    for h in heads:
        st_ref[h] = sts[h] * col(eglast, h) + _dot_tn(ks[h // rep] * col(kdec, h), v_news[h])
    for h in heads:
        o = outs[h]
        ms = jnp.mean(o * o, axis=-1, keepdims=True)
        on = o * lax.rsqrt(ms + NORM_EPS) * ng
        zg = _silu(z_ref[0, :, h * GDN_DV:(h + 1) * GDN_DV].astype(F32))
        o_ref[0, :, h * GDN_DV:(h + 1) * GDN_DV] = (on * zg).astype(BF16)


def _gdn_scan(main, gates, conv_w, a_log, dt_bias, norm_g):
    b, s, _ = main.shape
    c = GDN_CHUNK
    nconv = 2 * GDN_QK_HEADS * GDN_DK + GDN_V_HEADS * GDN_DV
    nv = GDN_V_HEADS * GDN_DV
    assert nconv == 2 * nv
    pad = jnp.zeros((GDN_V_HEADS,), F32)
    tail = jnp.zeros((LANES - 2 * GDN_V_HEADS,), F32)
    par = jnp.stack([jnp.concatenate([pad, a_log, tail]), jnp.concatenate([pad, dt_bias, tail])])
    return pl.pallas_call(
        _gdn_kernel,
        grid=(b, s // c),
        in_specs=[pl.BlockSpec((1, c, nconv), lambda i, t: (i, t, 0)),
                  pl.BlockSpec((1, c, nv), lambda i, t: (i, t, 2)),
                  pl.BlockSpec((1, c, LANES), lambda i, t: (i, t, 0)),
                  _const_spec((CONV_K, nconv)), _const_spec((2, LANES)), _const_spec((1, GDN_DV))],
        out_specs=pl.BlockSpec((1, c, nv), lambda i, t: (i, t, 0)),
        out_shape=jax.ShapeDtypeStruct((b, s, nv), BF16),
        scratch_shapes=[pltpu.VMEM((SUBLANES, nconv), F32),
                        pltpu.VMEM((GDN_V_HEADS, GDN_DK, GDN_DV), F32)],
        compiler_params=_params(("arbitrary", "arbitrary")),
        name="gdn_scan",
    )(main, main, gates, conv_w, par, norm_g.reshape(1, GDN_DV))


def _ssd_kernel(z_ref, x_ref, bc_ref, dt_ref, cw_ref, cb_ref, par_ref, ng_ref, o_ref, tail_ref,
                st_ref, *, heads):
    c = x_ref.shape[1]
    d_inner = x_ref.shape[2]
    p_dim = d_inner // heads
    gsz = SSD_STATE
    hpg = heads // SSD_GROUPS
    step = pl.program_id(1)

    @pl.when(step == 0)
    def _():
        tail_ref[...] = jnp.zeros_like(tail_ref)
        st_ref[...] = jnp.zeros_like(st_ref)

    xbc = jnp.concatenate([x_ref[0], bc_ref[0]], axis=1).astype(F32)
    xbc = _silu(_causal_conv(xbc, tail_ref, cw_ref[...]) + cb_ref[...])

    dt = _softplus(dt_ref[0] + par_ref[1:2, :])
    a = -jnp.exp(par_ref[0:1, :])
    acum = _cumsum_rows(dt * a)
    acum_t = jnp.concatenate([acum, jnp.zeros((LANES - c, LANES), F32)], axis=0).T if c < LANES \
        else acum.T
    ea = jnp.exp(acum)
    alast = acum[c - 1:c, :]
    ealast = jnp.exp(alast)
    wdec = jnp.exp(alast - acum)
    dskip = par_ref[2:3, :]
    incl, _ = _tri_masks(c)
    gw = hpg * p_dim
    head_of_lane = lax.broadcasted_iota(jnp.int32, (1, gw), 1) // p_dim
    groups = range(SSD_GROUPS)

    def expand(a, g):
        out = a[:, g * hpg:g * hpg + 1]
        for j in range(1, hpg):
            out = jnp.where(head_of_lane == j, a[:, g * hpg + j:g * hpg + j + 1], out)
        return jnp.broadcast_to(out, (a.shape[0], gw))

    bms = [xbc[:, d_inner + g * gsz:d_inner + (g + 1) * gsz] for g in groups]
    cms = [xbc[:, d_inner + (SSD_GROUPS + g) * gsz:d_inner + (SSD_GROUPS + g + 1) * gsz]
           for g in groups]
    xgs = [xbc[:, g * gw:(g + 1) * gw] for g in groups]
    cbs = [_dot_nt(cms[g], bms[g]) for g in groups]
    xdts = [xgs[g] * expand(dt, g) for g in groups]
    sts = [st_ref[g] for g in groups]

    ys = []
    for g in groups:
        lhs = jnp.concatenate(
            [cbs[g] * _decay_matrix(acum[:, h:h + 1], acum_t[h:h + 1, :c], incl)
             for h in range(g * hpg, (g + 1) * hpg)], axis=1)
        xb = xdts[g].astype(BF16)
        rhs = jnp.concatenate([jnp.where(head_of_lane == j, xb, jnp.zeros_like(xb))
                               for j in range(hpg)], axis=0)
        y = _dot(lhs, rhs) + _dot(cms[g], sts[g]) * expand(ea, g) + expand(dskip, g) * xgs[g]
        ys.append(y)
    for g in groups:
        st_ref[g] = sts[g] * expand(ealast, g) + _dot_tn(bms[g], xdts[g] * expand(wdec, g))
    for g in groups:
        sl = slice(g * gw, (g + 1) * gw)
        y = ys[g] * _silu(z_ref[0, :, sl].astype(F32))
        ms = jnp.mean(y * y, axis=-1, keepdims=True)
        o_ref[0, :, sl] = (y * lax.rsqrt(ms + NORM_EPS) * ng_ref[:, sl]).astype(BF16)


def _ssd_scan(main, dtraw, conv_w, conv_b, a_log, dt_bias, d_skip, norm_g, heads, d_inner):
    b, s, _ = main.shape
    c = min(SSD_CHUNK, s)
    nbc = 2 * SSD_GROUPS * SSD_STATE
    assert nbc == d_inner
    nconv = d_inner + nbc
    padl = lambda v: jnp.concatenate([v, jnp.zeros((LANES - heads,), F32)])
    par = jnp.stack([padl(a_log), padl(dt_bias), padl(d_skip)])
    tile = lambda j: pl.BlockSpec((1, c, d_inner), lambda i, t: (i, t, j))
    return pl.pallas_call(
        functools.partial(_ssd_kernel, heads=heads),
        grid=(b, s // c),
        in_specs=[tile(0), tile(1), tile(2),
                  pl.BlockSpec((1, c, LANES), lambda i, t: (i, t, 0)),
                  _const_spec((CONV_K, nconv)), _const_spec((1, nconv)), _const_spec((3, LANES)),
                  _const_spec((1, d_inner))],
        out_specs=tile(0),
        out_shape=jax.ShapeDtypeStruct((b, s, d_inner), BF16),
        scratch_shapes=[pltpu.VMEM((SUBLANES, nconv), F32),
                        pltpu.VMEM((SSD_GROUPS, SSD_STATE, d_inner // SSD_GROUPS), F32)],
        compiler_params=_params(("arbitrary", "arbitrary")),
        name="ssd_scan",
    )(main, main, main, dtraw, conv_w, conv_b.reshape(1, nconv), par, norm_g.reshape(1, d_inner))


def _deinterleave_heads(w, heads, dk):
    d = w.shape[0]
    return w.reshape(d, heads, dk // 2, 2).transpose(0, 1, 3, 2).reshape(d, heads * dk)


def _pad_gate_cols(w):
    d, n = w.shape
    return jnp.concatenate([w, jnp.zeros((d, LANES - n), w.dtype)], axis=1)


def kernel(x, c, ada_w, ada_b, norm_mix_g, norm_mlp_g, mlp_w1, mlp_w2, final_norm_g, ret_w_in, ret_w_out, gdn_w_in, gdn_conv_w, gdn_A_log, gdn_dt_bias, gdn_norm_g, gdn_w_out, ssd_w_in, ssd_conv_w, ssd_conv_b, ssd_A_log, ssd_dt_bias, ssd_D, ssd_norm_g, ssd_w_out):
    depth = ada_w.shape[0]
    b, s, d = x.shape
    mod = _modulation(c, ada_w, ada_b)

    for l in range(depth):
        sh1, sc1, gt1, sh2, sc2, gt2 = [mod[l, :, None, i * d:(i + 1) * d] for i in range(6)]
        kind = l % N_MIXERS
        j = l // N_MIXERS
        if kind == 0:
            w_in = ret_w_in[j]
            dk = d // RET_HEADS
            dv = 2 * d // RET_HEADS
            nqk = RET_HEADS * dk
            w_in = jnp.concatenate([_deinterleave_heads(w_in[:, :nqk], RET_HEADS, dk),
                                    _deinterleave_heads(w_in[:, nqk:2 * nqk], RET_HEADS, dk),
                                    w_in[:, 2 * nqk:]], axis=1).astype(BF16)
            proj, _ = _inproj(x, sh1, sc1, norm_mix_g[l], w_in)
            o = _retention_scan(proj, RET_HEADS, dk, dv)
            w_out = ret_w_out[j]
        elif kind == 1:
            w_in = gdn_w_in[j]
            nmain = 2 * GDN_QK_HEADS * GDN_DK + 2 * GDN_V_HEADS * GDN_DV
            proj, gates = _inproj(x, sh1, sc1, norm_mix_g[l], w_in[:, :nmain].astype(BF16),
                                  _pad_gate_cols(w_in[:, nmain:]).astype(BF16))
            o = _gdn_scan(proj, gates, gdn_conv_w[j], gdn_A_log[j], gdn_dt_bias[j], gdn_norm_g[j])
            w_out = gdn_w_out[j]
        else:
            w_in = ssd_w_in[j]
            d_inner = ssd_w_out.shape[1]
            heads = ssd_A_log.shape[1]
            nmain = 2 * d_inner + 2 * SSD_GROUPS * SSD_STATE
            proj, dtraw = _inproj(x, sh1, sc1, norm_mix_g[l], w_in[:, :nmain].astype(BF16),
                                  _pad_gate_cols(w_in[:, nmain:]).astype(BF16))
            o = _ssd_scan(proj, dtraw, ssd_conv_w[j], ssd_conv_b[j], ssd_A_log[j], ssd_dt_bias[j],
                          ssd_D[j], ssd_norm_g[j], heads, d_inner)
            w_out = ssd_w_out[j]
        x = _outmlp(o, x, gt1, sh2, sc2, gt2, norm_mlp_g[l], w_out.astype(BF16),
                    mlp_w1[l].astype(BF16), mlp_w2[l].astype(BF16),
                    gf=final_norm_g if l == depth - 1 else None)
    return x
```

```python
import functools
import math

import numpy as np
import jax
import jax.numpy as jnp
from jax import lax
from jax.experimental import pallas as pl
from jax.experimental.pallas import tpu as pltpu

F32 = jnp.float32
BF16 = jnp.bfloat16

NORM_EPS = 1e-6
CONV_K = 4
N_MIXERS = 3

RET_HEADS = 4
RET_ROPE_BASE = 10000.0
RET_CHUNK = 256

GDN_QK_HEADS = 8
GDN_V_HEADS = 16
GDN_DK = 128
GDN_DV = 128
GDN_CHUNK = 64
GDN_SOLVE_BLOCK = 16

SSD_HEAD_DIM = 64
SSD_GROUPS = 8
SSD_STATE = 128
SSD_CHUNK = 128

LANES = 128
SUBLANES = 8
VMEM_LIMIT_BYTES = 56 * 1024 * 1024

NT_DIMS = (((1,), (1,)), ((), ()))
TN_DIMS = (((0,), (0,)), ((), ()))


def _dot(a, b):
    return jnp.dot(a.astype(BF16), b.astype(BF16), preferred_element_type=F32)


def _dot_nt(a, b):
    return lax.dot_general(a.astype(BF16), b.astype(BF16), NT_DIMS, preferred_element_type=F32)


def _dot_tn(a, b):
    return lax.dot_general(a.astype(BF16), b.astype(BF16), TN_DIMS, preferred_element_type=F32)


def _silu(x):
    return x * jax.nn.sigmoid(x)


def _softplus(x):
    return jnp.maximum(x, 0.0) + jnp.log1p(jnp.exp(-jnp.abs(x)))


def _params(sem):
    return pltpu.CompilerParams(dimension_semantics=sem, vmem_limit_bytes=VMEM_LIMIT_BYTES)


def _const_spec(shape):
    nd = len(shape)
    return pl.BlockSpec(shape, lambda *_: (0,) * nd, pipeline_mode=pl.Buffered(1))


def _mod_kernel(c_ref, w_ref, b_ref, o_ref):
    cond = _silu(c_ref[...])
    o_ref[0] = jnp.dot(cond, w_ref[0], precision=lax.Precision.HIGHEST,
                       preferred_element_type=F32) + b_ref[0]


def _modulation(c, ada_w, ada_b):
    depth, d, d6 = ada_w.shape
    b = c.shape[0]
    return pl.pallas_call(
        _mod_kernel,
        grid=(depth, d6 // d),
        in_specs=[pl.BlockSpec((b, d), lambda l, j: (0, 0)),
                  pl.BlockSpec((1, d, d), lambda l, j: (l, 0, j)),
                  pl.BlockSpec((1, 1, d), lambda l, j: (l, 0, j))],
        out_specs=pl.BlockSpec((1, b, d), lambda l, j: (l, 0, j)),
        out_shape=jax.ShapeDtypeStruct((depth, b, d6), F32),
        compiler_params=_params(("arbitrary", "arbitrary")),
        name="adaln_mod",
    )(c, ada_w, ada_b.reshape(depth, 1, d6))


def _norm_mod(x, gamma, scale, shift):
    ms = jnp.mean(x * x, axis=-1, keepdims=True)
    return x * lax.rsqrt(ms + NORM_EPS) * gamma * (1.0 + scale) + shift


def _inproj_kernel(x_ref, sh_ref, sc_ref, g_ref, w_ref, *rest, n_chunk, has_gate):
    if has_gate:
        wg_ref, o_ref, og_ref = rest
    else:
        (o_ref,) = rest
    h = _norm_mod(x_ref[0], g_ref[...], sc_ref[0], sh_ref[0]).astype(BF16)
    n = w_ref.shape[1]
    for j in range(n // n_chunk):
        sl = slice(j * n_chunk, (j + 1) * n_chunk)
        o_ref[0, :, sl] = jnp.dot(h, w_ref[:, sl], preferred_element_type=F32).astype(BF16)
    if has_gate:
        og_ref[0] = jnp.dot(h, wg_ref[...], preferred_element_type=F32)


def _inproj(x, shift, scale, gamma, w, wg=None, tm=512, n_chunk=512):
    b, s, d = x.shape
    n = w.shape[1]
    tm = min(tm, s)
    has_gate = wg is not None
    row = pl.BlockSpec((1, 1, d), lambda i, t: (i, 0, 0))
    in_specs = [pl.BlockSpec((1, tm, d), lambda i, t: (i, t, 0)), row, row,
                _const_spec((1, d)), _const_spec((d, n))]
    args = [x, shift, scale, gamma.reshape(1, d), w]
    out_specs = [pl.BlockSpec((1, tm, n), lambda i, t: (i, t, 0))]
    out_shape = [jax.ShapeDtypeStruct((b, s, n), BF16)]
    if has_gate:
        in_specs.append(_const_spec((d, LANES)))
        args.append(wg)
        out_specs.append(pl.BlockSpec((1, tm, LANES), lambda i, t: (i, t, 0)))
        out_shape.append(jax.ShapeDtypeStruct((b, s, LANES), F32))
    out = pl.pallas_call(
        functools.partial(_inproj_kernel, n_chunk=n_chunk, has_gate=has_gate),
        grid=(b, s // tm),
        in_specs=in_specs, out_specs=out_specs, out_shape=out_shape,
        compiler_params=_params(("arbitrary", "arbitrary")),
        name="norm_inproj",
    )(*args)
    return out if has_gate else (out[0], None)


def _outmlp_kernel(o_ref, x_ref, gt1_ref, sh2_ref, sc2_ref, gt2_ref, g2_ref, wo_ref, w1_ref,
                   w2_ref, *rest, ff_chunk, final):
    if final:
        gf_ref, y_ref = rest
    else:
        (y_ref,) = rest
    y = jnp.dot(o_ref[0], wo_ref[...], preferred_element_type=F32)
    x1 = x_ref[0] + (1.0 + gt1_ref[0]) * y
    h2 = _norm_mod(x1, g2_ref[...], sc2_ref[0], sh2_ref[0]).astype(BF16)
    d_ff = w1_ref.shape[1]
    m = jnp.zeros_like(x1)
    for j in range(d_ff // ff_chunk):
        sl = slice(j * ff_chunk, (j + 1) * ff_chunk)
        u = jnp.maximum(jnp.dot(h2, w1_ref[:, sl], preferred_element_type=F32), 0.0)
        m = m + jnp.dot((u * u).astype(BF16), w2_ref[sl, :], preferred_element_type=F32)
    x2 = x1 + (1.0 + gt2_ref[0]) * m
    if final:
        ms = jnp.mean(x2 * x2, axis=-1, keepdims=True)
        x2 = x2 * lax.rsqrt(ms + NORM_EPS) * gf_ref[...]
    y_ref[0] = x2


def _outmlp(o, x, gt1, sh2, sc2, gt2, g2, w_out, w1, w2, gf=None, tm=512, ff_chunk=1024):
    b, s, d = x.shape
    dv = o.shape[-1]
    d_ff = w1.shape[1]
    tm = min(tm, s)
    final = gf is not None
    row = pl.BlockSpec((1, 1, d), lambda i, t: (i, 0, 0))
    tile = pl.BlockSpec((1, tm, d), lambda i, t: (i, t, 0))
    in_specs = [pl.BlockSpec((1, tm, dv), lambda i, t: (i, t, 0)), tile, row, row, row, row,
                _const_spec((1, d)), _const_spec((dv, d)), _const_spec((d, d_ff)),
                _const_spec((d_ff, d))]
    args = [o, x, gt1, sh2, sc2, gt2, g2.reshape(1, d), w_out, w1, w2]
    if final:
        in_specs.append(_const_spec((1, d)))
        args.append(gf.reshape(1, d))
    return pl.pallas_call(
        functools.partial(_outmlp_kernel, ff_chunk=ff_chunk, final=final),
        grid=(b, s // tm),
        in_specs=in_specs, out_specs=tile,
        out_shape=jax.ShapeDtypeStruct((b, s, d), F32),
        compiler_params=_params(("arbitrary", "arbitrary")),
        name="outproj_mlp",
    )(*args)


def _causal_conv(x, tail_ref, w):
    c = x.shape[0]
    xe = jnp.concatenate([tail_ref[...], x], axis=0)
    tail_ref[...] = x[c - SUBLANES:, :]
    acc = x * w[CONV_K - 1:CONV_K, :]
    for k in range(1, CONV_K):
        acc = acc + pltpu.roll(xe, k, 0)[SUBLANES:, :] * w[CONV_K - 1 - k:CONV_K - k, :]
    return acc


def _cumsum_rows(x):
    c = x.shape[0]
    row = lax.broadcasted_iota(jnp.int32, x.shape, 0)
    sh = 1
    while sh < c:
        x = x + jnp.where(row >= sh, pltpu.roll(x, sh, 0), 0.0)
        sh *= 2
    return x


def _tri_masks(c):
    t = lax.broadcasted_iota(jnp.int32, (c, c), 0)
    s = lax.broadcasted_iota(jnp.int32, (c, c), 1)
    return t >= s, t > s


def _decay_matrix(col, rowv, incl):
    return jnp.where(incl, jnp.exp(jnp.where(incl, col - rowv, 0.0)), 0.0)


def _rope_table_kernel(invf_ref, cos_ref, sin_ref):
    rows = cos_ref.shape[0]
    pos = (pl.program_id(0) * rows + lax.broadcasted_iota(jnp.int32, cos_ref.shape, 0)).astype(F32)
    ang = pos * invf_ref[...]
    cos_ref[...] = jnp.cos(ang)
    sin_ref[...] = jnp.sin(ang)


def _rope_table(s, half, rows=1024):
    rows = min(rows, s)
    inv_freq = RET_ROPE_BASE ** (-jnp.linspace(0.0, 1.0, half, dtype=F32))
    spec = pl.BlockSpec((rows, half), lambda t: (t, 0))
    return pl.pallas_call(
        _rope_table_kernel,
        grid=(s // rows,),
        in_specs=[_const_spec((1, half))],
        out_specs=[spec, spec],
        out_shape=[jax.ShapeDtypeStruct((s, half), F32)] * 2,
        compiler_params=_params(("arbitrary",)),
        name="rope_table",
    )(inv_freq.reshape(1, half))


def _ret_kernel(q_ref, k_ref, v_ref, g_ref, cos_ref, sin_ref, dmask_ref, dvec_ref, o_ref, r_ref, *,
                heads):
    dk = q_ref.shape[2] // heads
    dv = v_ref.shape[2] // heads
    half = dk // 2
    step = pl.program_id(1)

    @pl.when(step == 0)
    def _():
        r_ref[...] = jnp.zeros_like(r_ref)

    cs = cos_ref[...]
    sn = sin_ref[...]
    dvec = dvec_ref[...]

    def rope(ref, h):
        a = ref[0, :, h * dk:h * dk + half].astype(F32)
        b = ref[0, :, h * dk + half:(h + 1) * dk].astype(F32)
        return jnp.concatenate([a * cs - b * sn, a * sn + b * cs], axis=1)

    hs = range(heads)
    qhs = [rope(q_ref, h).astype(BF16) for h in hs]
    khs = [rope(k_ref, h) * (dk ** -0.5) for h in hs]
    vhs = [v_ref[0, :, h * dv:(h + 1) * dv] for h in hs]
    scs = [_dot_nt(qhs[h], khs[h]) * dmask_ref[h] for h in hs]
    rs = [r_ref[h] for h in hs]
    os_ = [_dot(scs[h], vhs[h]) + _dot(qhs[h], rs[h]) * dvec[:, h:h + 1] for h in hs]
    for h in hs:
        zeta = dvec[:, heads + h:heads + h + 1]
        dc = dvec[0:1, 2 * heads + h:2 * heads + h + 1]
        r_ref[h] = rs[h] * dc + _dot_tn(khs[h] * zeta, vhs[h])
    for h in hs:
        o = os_[h]
        ms = jnp.mean(o * o, axis=-1, keepdims=True)
        on = o * lax.rsqrt(ms + NORM_EPS)
        gate = _silu(g_ref[0, :, h * dv:(h + 1) * dv].astype(F32))
        o_ref[0, :, h * dv:(h + 1) * dv] = (gate * on).astype(BF16)


def _retention_scan(qkvg, rope_cos, rope_sin, heads, dk, dv):
    b, s, _ = qkvg.shape
    c = min(RET_CHUNK, s)
    nqk = heads * dk
    nv = heads * dv
    lg = jnp.log1p(-jnp.exp2(-5.0 - jnp.arange(heads, dtype=F32)))
    pos = jnp.arange(c, dtype=F32)
    rel = pos[:, None] - pos[None, :]
    causal = rel >= 0
    dmask = jnp.where(causal, jnp.exp(jnp.where(causal, rel, 0.0) * lg[:, None, None]), 0.0)
    xi = jnp.exp((pos[:, None] + 1.0) * lg[None, :])
    zeta = jnp.exp((c - 1.0 - pos)[:, None] * lg[None, :])
    dcs = jnp.broadcast_to(jnp.exp(c * lg)[None, :], (c, heads))
    dvec = jnp.concatenate([xi, zeta, dcs, jnp.zeros((c, LANES - 3 * heads), F32)], axis=1)
    assert nv == 2 * nqk
    rope_spec = pl.BlockSpec((c, dk // 2), lambda i, t: (t, 0))
    return pl.pallas_call(
        functools.partial(_ret_kernel, heads=heads),
        grid=(b, s // c),
        in_specs=[pl.BlockSpec((1, c, nqk), lambda i, t: (i, t, 0)),
                  pl.BlockSpec((1, c, nqk), lambda i, t: (i, t, 1)),
                  pl.BlockSpec((1, c, nv), lambda i, t: (i, t, 1)),
                  pl.BlockSpec((1, c, nv), lambda i, t: (i, t, 2)),
                  rope_spec, rope_spec, _const_spec((heads, c, c)), _const_spec((c, LANES))],
        out_specs=pl.BlockSpec((1, c, nv), lambda i, t: (i, t, 0)),
        out_shape=jax.ShapeDtypeStruct((b, s, nv), BF16),
        scratch_shapes=[pltpu.VMEM((heads, dk, dv), F32)],
        compiler_params=_params(("arbitrary", "arbitrary")),
        name="retention_scan",
    )(qkvg, qkvg, qkvg, qkvg, rope_cos, rope_sin, dmask, dvec)


def _unit_lower_solve(amats, rhs):
    c = amats[0].shape[0]
    blk = GDN_SOLVE_BLOCK
    row = lax.broadcasted_iota(jnp.int32, (c, c), 0)
    colm = lax.broadcasted_iota(jnp.int32, (c, c), 1)
    same = (row // blk) == (colm // blk)
    eye = (row == colm).astype(F32)
    ps = [jnp.where(same, -a, 0.0) for a in amats]
    offs = [jnp.where(same, 0.0, a) for a in amats]
    ts = [eye + p for p in ps]
    ps = [_dot(p, p) for p in ps]
    n_sq = int(math.log2(blk))
    for it in range(1, n_sq):
        if it + 1 < n_sq:
            tps = [_dot(jnp.concatenate([t, p], axis=0), p) for t, p in zip(ts, ps)]
            ts = [t + tp[:c] for t, tp in zip(ts, tps)]
            ps = [tp[c:] for tp in tps]
        else:
            ts = [t + _dot(t, p) for t, p in zip(ts, ps)]
    nps = [_dot(t, off) for t, off in zip(ts, offs)]
    xs = [_dot(t, r) for t, r in zip(ts, rhs)]
    for i in range(1, c // blk):
        lo, hi = i * blk, (i + 1) * blk
        xs = [jnp.concatenate([x[:lo], x[lo:hi] - _dot(n[lo:hi, :], x)] + ([x[hi:]] if hi < c else []),
                              axis=0) for n, x in zip(nps, xs)]
    return xs

def _gdn_kernel(qkv_ref, z_ref, gate_ref, cw_ref, par_ref, ng_ref, o_ref, tail_ref, st_ref):
    c = qkv_ref.shape[1]
    nqk = GDN_QK_HEADS * GDN_DK
    rep = GDN_V_HEADS // GDN_QK_HEADS
    step = pl.program_id(1)

    @pl.when(step == 0)
    def _():
        tail_ref[...] = jnp.zeros_like(tail_ref)
        st_ref[...] = jnp.zeros_like(st_ref)

    qkv = _silu(_causal_conv(qkv_ref[0].astype(F32), tail_ref, cw_ref[...]))

    gates = gate_ref[0]
    beta = jax.nn.sigmoid(gates)
    g = -jnp.exp(par_ref[0:1, :]) * _softplus(gates + par_ref[1:2, :])
    gc = _cumsum_rows(g)
    gct = jnp.concatenate([gc, jnp.zeros((LANES - c, LANES), F32)], axis=0).T
    eg = jnp.exp(gc)
    glast = gc[c - 1:c, :]
    eglast = jnp.exp(glast)
    kdec = jnp.exp(glast - gc)
    incl, strict = _tri_masks(c)
    ng = ng_ref[...]

    heads = range(GDN_V_HEADS)
    col = lambda a, h: a[:, GDN_V_HEADS + h:GDN_V_HEADS + h + 1]
    qs, ks, qkk = [], [], []
    for j in range(GDN_QK_HEADS):
        q = qkv[:, j * GDN_DK:(j + 1) * GDN_DK]
        k = qkv[:, nqk + j * GDN_DK:nqk + (j + 1) * GDN_DK]
        q = q * lax.rsqrt(jnp.sum(q * q, axis=-1, keepdims=True) + 1e-6) * (GDN_DK ** -0.5)
        k = k * lax.rsqrt(jnp.sum(k * k, axis=-1, keepdims=True) + 1e-6)
        qs.append(q)
        ks.append(k)
        qkk.append(_dot_nt(jnp.concatenate([q, k], axis=0), k))

    decays, amats, xs = [], [], []
    for h in heads:
        j = h // rep
        v = qkv[:, 2 * nqk + h * GDN_DV:2 * nqk + (h + 1) * GDN_DV]
        bcol = beta[:, h:h + 1]
        grow = gct[GDN_V_HEADS + h:GDN_V_HEADS + h + 1, :c]
        decay = _decay_matrix(col(gc, h), grow, incl)
        decays.append(decay)
        amats.append(jnp.where(strict, bcol * qkk[j][c:, :] * decay, 0.0))
        xs.append(jnp.concatenate([v * bcol, ks[j] * (bcol * col(eg, h))], axis=1))
    xs = _unit_lower_solve(amats, xs)

    sts = [st_ref[h] for h in heads]
    v_news = [xs[h][:, :GDN_DV] - _dot(xs[h][:, GDN_DV:], sts[h]) for h in heads]
    outs = []
    for h in heads:
        j = h // rep
        lhs = jnp.concatenate([qs[j] * col(eg, h), qkk[j][:c, :] * decays[h]], axis=1)
        outs.append(_dot(lhs, jnp.concatenate([sts[h], v_news[h]], axis=0)))
    for h in heads:
        st_ref[h] = sts[h] * col(eglast, h) + _dot_tn(ks[h // rep] * col(kdec, h), v_news[h])
    for h in heads:
        o = outs[h]
        ms = jnp.mean(o * o, axis=-1, keepdims=True)
        on = o * lax.rsqrt(ms + NORM_EPS) * ng
        zg = _silu(z_ref[0, :, h * GDN_DV:(h + 1) * GDN_DV].astype(F32))
        o_ref[0, :, h * GDN_DV:(h + 1) * GDN_DV] = (on * zg).astype(BF16)


def _gdn_scan(main, gates, conv_w, a_log, dt_bias, norm_g):
    b, s, _ = main.shape
    c = GDN_CHUNK
    nconv = 2 * GDN_QK_HEADS * GDN_DK + GDN_V_HEADS * GDN_DV
    nv = GDN_V_HEADS * GDN_DV
    assert nconv == 2 * nv
    pad = jnp.zeros((GDN_V_HEADS,), F32)
    tail = jnp.zeros((LANES - 2 * GDN_V_HEADS,), F32)
    par = jnp.stack([jnp.concatenate([pad, a_log, tail]), jnp.concatenate([pad, dt_bias, tail])])
    return pl.pallas_call(
        _gdn_kernel,
        grid=(b, s // c),
        in_specs=[pl.BlockSpec((1, c, nconv), lambda i, t: (i, t, 0)),
                  pl.BlockSpec((1, c, nv), lambda i, t: (i, t, 2)),
                  pl.BlockSpec((1, c, LANES), lambda i, t: (i, t, 0)),
                  _const_spec((CONV_K, nconv)), _const_spec((2, LANES)), _const_spec((1, GDN_DV))],
        out_specs=pl.BlockSpec((1, c, nv), lambda i, t: (i, t, 0)),
        out_shape=jax.ShapeDtypeStruct((b, s, nv), BF16),
        scratch_shapes=[pltpu.VMEM((SUBLANES, nconv), F32),
                        pltpu.VMEM((GDN_V_HEADS, GDN_DK, GDN_DV), F32)],
        compiler_params=_params(("arbitrary", "arbitrary")),
        name="gdn_scan",
    )(main, main, gates, conv_w, par, norm_g.reshape(1, GDN_DV))


def _ssd_kernel(z_ref, x_ref, bc_ref, dt_ref, cw_ref, cb_ref, par_ref, ng_ref, o_ref, tail_ref,
                st_ref, *, heads):
    c = x_ref.shape[1]
    d_inner = x_ref.shape[2]
    p_dim = d_inner // heads
    gsz = SSD_STATE
    hpg = heads // SSD_GROUPS
    step = pl.program_id(1)

    @pl.when(step == 0)
    def _():
        tail_ref[...] = jnp.zeros_like(tail_ref)
        st_ref[...] = jnp.zeros_like(st_ref)

    xbc = jnp.concatenate([x_ref[0], bc_ref[0]], axis=1).astype(F32)
    xbc = _silu(_causal_conv(xbc, tail_ref, cw_ref[...]) + cb_ref[...])

    dt = _softplus(dt_ref[0] + par_ref[1:2, :])
    a = -jnp.exp(par_ref[0:1, :])
    acum = _cumsum_rows(dt * a)
    acum_t = jnp.concatenate([acum, jnp.zeros((LANES - c, LANES), F32)], axis=0).T if c < LANES \
        else acum.T
    ea = jnp.exp(acum)
    alast = acum[c - 1:c, :]
    ealast = jnp.exp(alast)
    wdec = jnp.exp(alast - acum)
    dskip = par_ref[2:3, :]
    incl, _ = _tri_masks(c)
    gw = hpg * p_dim
    head_of_lane = lax.broadcasted_iota(jnp.int32, (1, gw), 1) // p_dim
    groups = range(SSD_GROUPS)

    def expand(a, g):
        out = a[:, g * hpg:g * hpg + 1]
        for j in range(1, hpg):
            out = jnp.where(head_of_lane == j, a[:, g * hpg + j:g * hpg + j + 1], out)
        return jnp.broadcast_to(out, (a.shape[0], gw))

    bms = [xbc[:, d_inner + g * gsz:d_inner + (g + 1) * gsz] for g in groups]
    cms = [xbc[:, d_inner + (SSD_GROUPS + g) * gsz:d_inner + (SSD_GROUPS + g + 1) * gsz]
           for g in groups]
    xgs = [xbc[:, g * gw:(g + 1) * gw] for g in groups]
    cbs = [_dot_nt(cms[g], bms[g]) for g in groups]
    xdts = [xgs[g] * expand(dt, g) for g in groups]
    sts = [st_ref[g] for g in groups]

    ys = []
    for g in groups:
        lhs = jnp.concatenate(
            [cbs[g] * _decay_matrix(acum[:, h:h + 1], acum_t[h:h + 1, :c], incl)
             for h in range(g * hpg, (g + 1) * hpg)], axis=1)
        xb = xdts[g].astype(BF16)
        rhs = jnp.concatenate([jnp.where(head_of_lane == j, xb, jnp.zeros_like(xb))
                               for j in range(hpg)], axis=0)
        y = _dot(lhs, rhs) + _dot(cms[g], sts[g]) * expand(ea, g) + expand(dskip, g) * xgs[g]
        ys.append(y)
    for g in groups:
        st_ref[g] = sts[g] * expand(ealast, g) + _dot_tn(bms[g], xdts[g] * expand(wdec, g))
    for g in groups:
        sl = slice(g * gw, (g + 1) * gw)
        y = ys[g] * _silu(z_ref[0, :, sl].astype(F32))
        ms = jnp.mean(y * y, axis=-1, keepdims=True)
        o_ref[0, :, sl] = (y * lax.rsqrt(ms + NORM_EPS) * ng_ref[:, sl]).astype(BF16)


def _ssd_scan(main, dtraw, conv_w, conv_b, a_log, dt_bias, d_skip, norm_g, heads, d_inner):
    b, s, _ = main.shape
    c = min(SSD_CHUNK, s)
    nbc = 2 * SSD_GROUPS * SSD_STATE
    assert nbc == d_inner
    nconv = d_inner + nbc
    padl = lambda v: jnp.concatenate([v, jnp.zeros((LANES - heads,), F32)])
    par = jnp.stack([padl(a_log), padl(dt_bias), padl(d_skip)])
    tile = lambda j: pl.BlockSpec((1, c, d_inner), lambda i, t: (i, t, j))
    return pl.pallas_call(
        functools.partial(_ssd_kernel, heads=heads),
        grid=(b, s // c),
        in_specs=[tile(0), tile(1), tile(2),
                  pl.BlockSpec((1, c, LANES), lambda i, t: (i, t, 0)),
                  _const_spec((CONV_K, nconv)), _const_spec((1, nconv)), _const_spec((3, LANES)),
                  _const_spec((1, d_inner))],
        out_specs=tile(0),
        out_shape=jax.ShapeDtypeStruct((b, s, d_inner), BF16),
        scratch_shapes=[pltpu.VMEM((SUBLANES, nconv), F32),
                        pltpu.VMEM((SSD_GROUPS, SSD_STATE, d_inner // SSD_GROUPS), F32)],
        compiler_params=_params(("arbitrary", "arbitrary")),
        name="ssd_scan",
    )(main, main, main, dtraw, conv_w, conv_b.reshape(1, nconv), par, norm_g.reshape(1, d_inner))


def _deinterleave_heads(w, heads, dk):
    d = w.shape[0]
    return w.reshape(d, heads, dk // 2, 2).transpose(0, 1, 3, 2).reshape(d, heads * dk)


def _pad_gate_cols(w):
    d, n = w.shape
    return jnp.concatenate([w, jnp.zeros((d, LANES - n), w.dtype)], axis=1)


def kernel(x, c, ada_w, ada_b, norm_mix_g, norm_mlp_g, mlp_w1, mlp_w2, final_norm_g, ret_w_in, ret_w_out, gdn_w_in, gdn_conv_w, gdn_A_log, gdn_dt_bias, gdn_norm_g, gdn_w_out, ssd_w_in, ssd_conv_w, ssd_conv_b, ssd_A_log, ssd_dt_bias, ssd_D, ssd_norm_g, ssd_w_out):
    depth = ada_w.shape[0]
    b, s, d = x.shape
    mod = _modulation(c, ada_w, ada_b)
    rope_cos, rope_sin = _rope_table(s, d // RET_HEADS // 2)

    for l in range(depth):
        sh1, sc1, gt1, sh2, sc2, gt2 = [mod[l, :, None, i * d:(i + 1) * d] for i in range(6)]
        kind = l % N_MIXERS
        j = l // N_MIXERS
        if kind == 0:
            w_in = ret_w_in[j]
            dk = d // RET_HEADS
            dv = 2 * d // RET_HEADS
            nqk = RET_HEADS * dk
            w_in = jnp.concatenate([_deinterleave_heads(w_in[:, :nqk], RET_HEADS, dk),
                                    _deinterleave_heads(w_in[:, nqk:2 * nqk], RET_HEADS, dk),
                                    w_in[:, 2 * nqk:]], axis=1).astype(BF16)
            proj, _ = _inproj(x, sh1, sc1, norm_mix_g[l], w_in)
            o = _retention_scan(proj, rope_cos, rope_sin, RET_HEADS, dk, dv)
            w_out = ret_w_out[j]
        elif kind == 1:
            w_in = gdn_w_in[j]
            nmain = 2 * GDN_QK_HEADS * GDN_DK + 2 * GDN_V_HEADS * GDN_DV
            proj, gates = _inproj(x, sh1, sc1, norm_mix_g[l], w_in[:, :nmain].astype(BF16),
                                  _pad_gate_cols(w_in[:, nmain:]).astype(BF16))
            o = _gdn_scan(proj, gates, gdn_conv_w[j], gdn_A_log[j], gdn_dt_bias[j], gdn_norm_g[j])
            w_out = gdn_w_out[j]
        else:
            w_in = ssd_w_in[j]
            d_inner = ssd_w_out.shape[1]
            heads = ssd_A_log.shape[1]
            nmain = 2 * d_inner + 2 * SSD_GROUPS * SSD_STATE
            proj, dtraw = _inproj(x, sh1, sc1, norm_mix_g[l], w_in[:, :nmain].astype(BF16),
                                  _pad_gate_cols(w_in[:, nmain:]).astype(BF16))
            o = _ssd_scan(proj, dtraw, ssd_conv_w[j], ssd_conv_b[j], ssd_A_log[j], ssd_dt_bias[j],
                          ssd_D[j], ssd_norm_g[j], heads, d_inner)
            w_out = ssd_w_out[j]
        x = _outmlp(o, x, gt1, sh2, sc2, gt2, norm_mlp_g[l], w_out.astype(BF16),
                    mlp_w1[l].astype(BF16), mlp_w2[l].astype(BF16),
                    gf=final_norm_g if l == depth - 1 else None)
    return x
```

```python
import functools
import math

import numpy as np
import jax
import jax.numpy as jnp
from jax import lax
from jax.experimental import pallas as pl
from jax.experimental.pallas import tpu as pltpu

F32 = jnp.float32
BF16 = jnp.bfloat16

NORM_EPS = 1e-6
CONV_K = 4
N_MIXERS = 3

RET_HEADS = 4
RET_ROPE_BASE = 10000.0
RET_CHUNK = 256

GDN_QK_HEADS = 8
GDN_V_HEADS = 16
GDN_DK = 128
GDN_DV = 128
GDN_CHUNK = 64
GDN_SOLVE_BLOCK = 16
GDN_SEQS_PER_STEP = 2

SSD_HEAD_DIM = 64
SSD_GROUPS = 8
SSD_STATE = 128
SSD_CHUNK = 128

LANES = 128
SUBLANES = 8
VMEM_LIMIT_BYTES = 56 * 1024 * 1024

NT_DIMS = (((1,), (1,)), ((), ()))
TN_DIMS = (((0,), (0,)), ((), ()))


def _dot(a, b):
    return jnp.dot(a.astype(BF16), b.astype(BF16), preferred_element_type=F32)


def _dot_nt(a, b):
    return lax.dot_general(a.astype(BF16), b.astype(BF16), NT_DIMS, preferred_element_type=F32)


def _dot_tn(a, b):
    return lax.dot_general(a.astype(BF16), b.astype(BF16), TN_DIMS, preferred_element_type=F32)


def _silu(x):
    return x * jax.nn.sigmoid(x)


def _softplus(x):
    return jnp.maximum(x, 0.0) + jnp.log1p(jnp.exp(-jnp.abs(x)))


def _params(sem):
    return pltpu.CompilerParams(dimension_semantics=sem, vmem_limit_bytes=VMEM_LIMIT_BYTES)


def _const_spec(shape):
    nd = len(shape)
    return pl.BlockSpec(shape, lambda *_: (0,) * nd, pipeline_mode=pl.Buffered(1))


def _mod_kernel(c_ref, w_ref, b_ref, o_ref):
    cond = _silu(c_ref[...])
    o_ref[0] = jnp.dot(cond, w_ref[0], precision=lax.Precision.HIGHEST,
                       preferred_element_type=F32) + b_ref[0]


def _modulation(c, ada_w, ada_b):
    depth, d, d6 = ada_w.shape
    b = c.shape[0]
    return pl.pallas_call(
        _mod_kernel,
        grid=(depth, d6 // d),
        in_specs=[pl.BlockSpec((b, d), lambda l, j: (0, 0)),
                  pl.BlockSpec((1, d, d), lambda l, j: (l, 0, j)),
                  pl.BlockSpec((1, 1, d), lambda l, j: (l, 0, j))],
        out_specs=pl.BlockSpec((1, b, d), lambda l, j: (l, 0, j)),
        out_shape=jax.ShapeDtypeStruct((depth, b, d6), F32),
        compiler_params=_params(("arbitrary", "arbitrary")),
        name="adaln_mod",
    )(c, ada_w, ada_b.reshape(depth, 1, d6))


def _norm_mod(x, gamma, scale, shift):
    ms = jnp.mean(x * x, axis=-1, keepdims=True)
    return x * lax.rsqrt(ms + NORM_EPS) * gamma * (1.0 + scale) + shift


def _inproj_kernel(x_ref, sh_ref, sc_ref, g_ref, w_ref, *rest, n_chunk, has_gate):
    if has_gate:
        wg_ref, o_ref, og_ref = rest
    else:
        (o_ref,) = rest
    h = _norm_mod(x_ref[0], g_ref[...], sc_ref[0], sh_ref[0]).astype(BF16)
    n = w_ref.shape[1]
    for j in range(n // n_chunk):
        sl = slice(j * n_chunk, (j + 1) * n_chunk)
        o_ref[0, :, sl] = jnp.dot(h, w_ref[:, sl], preferred_element_type=F32).astype(BF16)
    if has_gate:
        og_ref[0] = jnp.dot(h, wg_ref[...], preferred_element_type=F32)


def _inproj(x, shift, scale, gamma, w, wg=None, tm=1024, n_chunk=512):
    b, s, d = x.shape
    n = w.shape[1]
    tm = min(tm, s)
    has_gate = wg is not None
    row = pl.BlockSpec((1, 1, d), lambda i, t: (i, 0, 0))
    in_specs = [pl.BlockSpec((1, tm, d), lambda i, t: (i, t, 0)), row, row,
                _const_spec((1, d)), _const_spec((d, n))]
    args = [x, shift, scale, gamma.reshape(1, d), w]
    out_specs = [pl.BlockSpec((1, tm, n), lambda i, t: (i, t, 0))]
    out_shape = [jax.ShapeDtypeStruct((b, s, n), BF16)]
    if has_gate:
        in_specs.append(_const_spec((d, LANES)))
        args.append(wg)
        out_specs.append(pl.BlockSpec((1, tm, LANES), lambda i, t: (i, t, 0)))
        out_shape.append(jax.ShapeDtypeStruct((b, s, LANES), F32))
    out = pl.pallas_call(
        functools.partial(_inproj_kernel, n_chunk=n_chunk, has_gate=has_gate),
        grid=(b, s // tm),
        in_specs=in_specs, out_specs=out_specs, out_shape=out_shape,
        compiler_params=_params(("arbitrary", "arbitrary")),
        name="norm_inproj",
    )(*args)
    return out if has_gate else (out[0], None)


def _outmlp_kernel(o_ref, x_ref, gt1_ref, sh2_ref, sc2_ref, gt2_ref, g2_ref, wo_ref, w1_ref,
                   w2_ref, *rest, ff_chunk, final):
    if final:
        gf_ref, y_ref = rest
    else:
        (y_ref,) = rest
    y = jnp.dot(o_ref[0], wo_ref[...], preferred_element_type=F32)
    x1 = x_ref[0] + (1.0 + gt1_ref[0]) * y
    h2 = _norm_mod(x1, g2_ref[...], sc2_ref[0], sh2_ref[0]).astype(BF16)
    d_ff = w1_ref.shape[1]
    m = jnp.zeros_like(x1)
    for j in range(d_ff // ff_chunk):
        sl = slice(j * ff_chunk, (j + 1) * ff_chunk)
        u = jnp.maximum(jnp.dot(h2, w1_ref[:, sl], preferred_element_type=F32), 0.0)
        m = m + jnp.dot((u * u).astype(BF16), w2_ref[sl, :], preferred_element_type=F32)
    x2 = x1 + (1.0 + gt2_ref[0]) * m
    if final:
        ms = jnp.mean(x2 * x2, axis=-1, keepdims=True)
        x2 = x2 * lax.rsqrt(ms + NORM_EPS) * gf_ref[...]
    y_ref[0] = x2


def _outmlp(o, x, gt1, sh2, sc2, gt2, g2, w_out, w1, w2, gf=None, tm=512, ff_chunk=1024):
    b, s, d = x.shape
    dv = o.shape[-1]
    d_ff = w1.shape[1]
    tm = min(tm, s)
    final = gf is not None
    row = pl.BlockSpec((1, 1, d), lambda i, t: (i, 0, 0))
    tile = pl.BlockSpec((1, tm, d), lambda i, t: (i, t, 0))
    in_specs = [pl.BlockSpec((1, tm, dv), lambda i, t: (i, t, 0)), tile, row, row, row, row,
                _const_spec((1, d)), _const_spec((dv, d)), _const_spec((d, d_ff)),
                _const_spec((d_ff, d))]
    args = [o, x, gt1, sh2, sc2, gt2, g2.reshape(1, d), w_out, w1, w2]
    if final:
        in_specs.append(_const_spec((1, d)))
        args.append(gf.reshape(1, d))
    return pl.pallas_call(
        functools.partial(_outmlp_kernel, ff_chunk=ff_chunk, final=final),
        grid=(b, s // tm),
        in_specs=in_specs, out_specs=tile,
        out_shape=jax.ShapeDtypeStruct((b, s, d), F32),
        compiler_params=_params(("arbitrary", "arbitrary")),
        name="outproj_mlp",
    )(*args)


def _causal_conv(x, tail_ref, w):
    c = x.shape[0]
    xe = jnp.concatenate([tail_ref[...], x], axis=0)
    tail_ref[...] = x[c - SUBLANES:, :]
    acc = x * w[CONV_K - 1:CONV_K, :]
    for k in range(1, CONV_K):
        acc = acc + pltpu.roll(xe, k, 0)[SUBLANES:, :] * w[CONV_K - 1 - k:CONV_K - k, :]
    return acc


def _cumsum_rows(x):
    c = x.shape[0]
    row = lax.broadcasted_iota(jnp.int32, x.shape, 0)
    sh = 1
    while sh < c:
        x = x + jnp.where(row >= sh, pltpu.roll(x, sh, 0), 0.0)
        sh *= 2
    return x


def _tri_masks(c):
    t = lax.broadcasted_iota(jnp.int32, (c, c), 0)
    s = lax.broadcasted_iota(jnp.int32, (c, c), 1)
    return t >= s, t > s


def _decay_matrix(col, rowv, incl):
    return jnp.where(incl, jnp.exp(jnp.where(incl, col - rowv, 0.0)), 0.0)


def _rope_table_kernel(invf_ref, cos_ref, sin_ref):
    rows = cos_ref.shape[0]
    pos = (pl.program_id(0) * rows + lax.broadcasted_iota(jnp.int32, cos_ref.shape, 0)).astype(F32)
    ang = pos * invf_ref[...]
    cos_ref[...] = jnp.cos(ang)
    sin_ref[...] = jnp.sin(ang)


def _rope_table(s, half, rows=1024):
    rows = min(rows, s)
    inv_freq = RET_ROPE_BASE ** (-jnp.linspace(0.0, 1.0, half, dtype=F32))
    spec = pl.BlockSpec((rows, half), lambda t: (t, 0))
    return pl.pallas_call(
        _rope_table_kernel,
        grid=(s // rows,),
        in_specs=[_const_spec((1, half))],
        out_specs=[spec, spec],
        out_shape=[jax.ShapeDtypeStruct((s, half), F32)] * 2,
        compiler_params=_params(("arbitrary",)),
        name="rope_table",
    )(inv_freq.reshape(1, half))


def _ret_kernel(q_ref, k_ref, v_ref, g_ref, cos_ref, sin_ref, dmask_ref, dvec_ref, o_ref, r_ref, *,
                heads):
    dk = q_ref.shape[2] // heads
    dv = v_ref.shape[2] // heads
    half = dk // 2
    step = pl.program_id(1)

    @pl.when(step == 0)
    def _():
        r_ref[...] = jnp.zeros_like(r_ref)

    cs = cos_ref[...]
    sn = sin_ref[...]
    dvec = dvec_ref[...]

    def rope(ref, h):
        a = ref[0, :, h * dk:h * dk + half].astype(F32)
        b = ref[0, :, h * dk + half:(h + 1) * dk].astype(F32)
        return jnp.concatenate([a * cs - b * sn, a * sn + b * cs], axis=1)

    hs = range(heads)
    qhs = [rope(q_ref, h).astype(BF16) for h in hs]
    khs = [rope(k_ref, h) * (dk ** -0.5) for h in hs]
    vhs = [v_ref[0, :, h * dv:(h + 1) * dv] for h in hs]
    scs = [_dot_nt(qhs[h], khs[h]) * dmask_ref[h] for h in hs]
    rs = [r_ref[h] for h in hs]
    os_ = [_dot(scs[h], vhs[h]) + _dot(qhs[h], rs[h]) * dvec[:, h:h + 1] for h in hs]
    for h in hs:
        zeta = dvec[:, heads + h:heads + h + 1]
        dc = dvec[0:1, 2 * heads + h:2 * heads + h + 1]
        r_ref[h] = rs[h] * dc + _dot_tn(khs[h] * zeta, vhs[h])
    for h in hs:
        o = os_[h]
        ms = jnp.mean(o * o, axis=-1, keepdims=True)
        on = o * lax.rsqrt(ms + NORM_EPS)
        gate = _silu(g_ref[0, :, h * dv:(h + 1) * dv].astype(F32))
        o_ref[0, :, h * dv:(h + 1) * dv] = (gate * on).astype(BF16)


def _retention_scan(qkvg, rope_cos, rope_sin, heads, dk, dv):
    b, s, _ = qkvg.shape
    c = min(RET_CHUNK, s)
    nqk = heads * dk
    nv = heads * dv
    lg = jnp.log1p(-jnp.exp2(-5.0 - jnp.arange(heads, dtype=F32)))
    pos = jnp.arange(c, dtype=F32)
    rel = pos[:, None] - pos[None, :]
    causal = rel >= 0
    dmask = jnp.where(causal, jnp.exp(jnp.where(causal, rel, 0.0) * lg[:, None, None]), 0.0)
    xi = jnp.exp((pos[:, None] + 1.0) * lg[None, :])
    zeta = jnp.exp((c - 1.0 - pos)[:, None] * lg[None, :])
    dcs = jnp.broadcast_to(jnp.exp(c * lg)[None, :], (c, heads))
    dvec = jnp.concatenate([xi, zeta, dcs, jnp.zeros((c, LANES - 3 * heads), F32)], axis=1)
    assert nv == 2 * nqk
    rope_spec = pl.BlockSpec((c, dk // 2), lambda i, t: (t, 0))
    return pl.pallas_call(
        functools.partial(_ret_kernel, heads=heads),
        grid=(b, s // c),
        in_specs=[pl.BlockSpec((1, c, nqk), lambda i, t: (i, t, 0)),
                  pl.BlockSpec((1, c, nqk), lambda i, t: (i, t, 1)),
                  pl.BlockSpec((1, c, nv), lambda i, t: (i, t, 1)),
                  pl.BlockSpec((1, c, nv), lambda i, t: (i, t, 2)),
                  rope_spec, rope_spec, _const_spec((heads, c, c)), _const_spec((c, LANES))],
        out_specs=pl.BlockSpec((1, c, nv), lambda i, t: (i, t, 0)),
        out_shape=jax.ShapeDtypeStruct((b, s, nv), BF16),
        scratch_shapes=[pltpu.VMEM((heads, dk, dv), F32)],
        compiler_params=_params(("arbitrary", "arbitrary")),
        name="retention_scan",
    )(qkvg, qkvg, qkvg, qkvg, rope_cos, rope_sin, dmask, dvec)


def _unit_lower_solve(amats, rhs):
    c = amats[0].shape[0]
    blk = GDN_SOLVE_BLOCK
    row = lax.broadcasted_iota(jnp.int32, (c, c), 0)
    colm = lax.broadcasted_iota(jnp.int32, (c, c), 1)
    same = (row // blk) == (colm // blk)
    eye = (row == colm).astype(F32)
    ps = [jnp.where(same, -a, 0.0) for a in amats]
    offs = [jnp.where(same, 0.0, a) for a in amats]
    ts = [eye + p for p in ps]
    ps = [_dot(p, p) for p in ps]
    n_sq = int(math.log2(blk))
    for it in range(1, n_sq):
        if it + 1 < n_sq:
            tps = [_dot(jnp.concatenate([t, p], axis=0), p) for t, p in zip(ts, ps)]
            ts = [t + tp[:c] for t, tp in zip(ts, tps)]
            ps = [tp[c:] for tp in tps]
        else:
            ts = [t + _dot(t, p) for t, p in zip(ts, ps)]
    nps = [_dot(t, off) for t, off in zip(ts, offs)]
    xs = [_dot(t, r) for t, r in zip(ts, rhs)]
    for i in range(1, c // blk):
        lo, hi = i * blk, (i + 1) * blk
        xs = [jnp.concatenate([x[:lo], x[lo:hi] - _dot(n[lo:hi, :], x)] + ([x[hi:]] if hi < c else []),
                              axis=0) for n, x in zip(nps, xs)]
    return xs

def _gdn_kernel(qkv_ref, z_ref, gate_ref, cw_ref, par_ref, ng_ref, o_ref, tail_ref, st_ref):
    c = qkv_ref.shape[1]
    nqk = GDN_QK_HEADS * GDN_DK
    rep = GDN_V_HEADS // GDN_QK_HEADS
    step = pl.program_id(1)

    @pl.when(step == 0)
    def _():
        tail_ref[...] = jnp.zeros_like(tail_ref)
        st_ref[...] = jnp.zeros_like(st_ref)

    incl, strict = _tri_masks(c)
    ng = ng_ref[...]
    heads = range(GDN_V_HEADS)
    col = lambda a, h: a[:, GDN_V_HEADS + h:GDN_V_HEADS + h + 1]

    def one_sequence(bi):
        qkv = _silu(_causal_conv(qkv_ref[bi].astype(F32), tail_ref.at[bi], cw_ref[...]))

        gates = gate_ref[bi]
        beta = jax.nn.sigmoid(gates)
        g = -jnp.exp(par_ref[0:1, :]) * _softplus(gates + par_ref[1:2, :])
        gc = _cumsum_rows(g)
        gct = jnp.concatenate([gc, jnp.zeros((LANES - c, LANES), F32)], axis=0).T
        eg = jnp.exp(gc)
        glast = gc[c - 1:c, :]
        eglast = jnp.exp(glast)
        kdec = jnp.exp(glast - gc)

        qs, ks, qkk = [], [], []
        for j in range(GDN_QK_HEADS):
            q = qkv[:, j * GDN_DK:(j + 1) * GDN_DK]
            k = qkv[:, nqk + j * GDN_DK:nqk + (j + 1) * GDN_DK]
            q = q * lax.rsqrt(jnp.sum(q * q, axis=-1, keepdims=True) + 1e-6) * (GDN_DK ** -0.5)
            k = k * lax.rsqrt(jnp.sum(k * k, axis=-1, keepdims=True) + 1e-6)
            qs.append(q)
            ks.append(k)
            qkk.append(_dot_nt(jnp.concatenate([q, k], axis=0), k))

        decays, amats, xs = [], [], []
        for h in heads:
            j = h // rep
            v = qkv[:, 2 * nqk + h * GDN_DV:2 * nqk + (h + 1) * GDN_DV]
            bcol = beta[:, h:h + 1]
            grow = gct[GDN_V_HEADS + h:GDN_V_HEADS + h + 1, :c]
            decay = _decay_matrix(col(gc, h), grow, incl)
            decays.append(decay)
            amats.append(jnp.where(strict, bcol * qkk[j][c:, :] * decay, 0.0))
            xs.append(jnp.concatenate([v * bcol, ks[j] * (bcol * col(eg, h))], axis=1))
        xs = _unit_lower_solve(amats, xs)

        sts = [st_ref[bi * GDN_V_HEADS + h] for h in heads]
        v_news = [xs[h][:, :GDN_DV] - _dot(xs[h][:, GDN_DV:], sts[h]) for h in heads]
        outs = []
        for h in heads:
            j = h // rep
            lhs = jnp.concatenate([qs[j] * col(eg, h), qkk[j][:c, :] * decays[h]], axis=1)
            outs.append(_dot(lhs, jnp.concatenate([sts[h], v_news[h]], axis=0)))
        for h in heads:
            st_ref[bi * GDN_V_HEADS + h] = sts[h] * col(eglast, h) + _dot_tn(
                ks[h // rep] * col(kdec, h), v_news[h])
        for h in heads:
            o = outs[h]
            ms = jnp.mean(o * o, axis=-1, keepdims=True)
            on = o * lax.rsqrt(ms + NORM_EPS) * ng
            zg = _silu(z_ref[bi, :, h * GDN_DV:(h + 1) * GDN_DV].astype(F32))
            o_ref[bi, :, h * GDN_DV:(h + 1) * GDN_DV] = (on * zg).astype(BF16)

    for bi in range(qkv_ref.shape[0]):
        one_sequence(bi)


def _gdn_scan(main, gates, conv_w, a_log, dt_bias, norm_g):
    b, s, _ = main.shape
    c = GDN_CHUNK
    nconv = 2 * GDN_QK_HEADS * GDN_DK + GDN_V_HEADS * GDN_DV
    nv = GDN_V_HEADS * GDN_DV
    assert nconv == 2 * nv
    pad = jnp.zeros((GDN_V_HEADS,), F32)
    tail = jnp.zeros((LANES - 2 * GDN_V_HEADS,), F32)
    par = jnp.stack([jnp.concatenate([pad, a_log, tail]), jnp.concatenate([pad, dt_bias, tail])])
    nb = GDN_SEQS_PER_STEP if b % GDN_SEQS_PER_STEP == 0 else 1
    return pl.pallas_call(
        _gdn_kernel,
        grid=(b // nb, s // c),
        in_specs=[pl.BlockSpec((nb, c, nconv), lambda i, t: (i, t, 0)),
                  pl.BlockSpec((nb, c, nv), lambda i, t: (i, t, 2)),
                  pl.BlockSpec((nb, c, LANES), lambda i, t: (i, t, 0)),
                  _const_spec((CONV_K, nconv)), _const_spec((2, LANES)), _const_spec((1, GDN_DV))],
        out_specs=pl.BlockSpec((nb, c, nv), lambda i, t: (i, t, 0)),
        out_shape=jax.ShapeDtypeStruct((b, s, nv), BF16),
        scratch_shapes=[pltpu.VMEM((nb, SUBLANES, nconv), F32),
                        pltpu.VMEM((nb * GDN_V_HEADS, GDN_DK, GDN_DV), F32)],
        compiler_params=_params(("arbitrary", "arbitrary")),
        name="gdn_scan",
    )(main, main, gates, conv_w, par, norm_g.reshape(1, GDN_DV))


def _ssd_kernel(z_ref, x_ref, bc_ref, dt_ref, cw_ref, cb_ref, par_ref, ng_ref, o_ref, tail_ref,
                st_ref, *, heads):
    c = x_ref.shape[1]
    d_inner = x_ref.shape[2]
    p_dim = d_inner // heads
    gsz = SSD_STATE
    hpg = heads // SSD_GROUPS
    step = pl.program_id(1)

    @pl.when(step == 0)
    def _():
        tail_ref[...] = jnp.zeros_like(tail_ref)
        st_ref[...] = jnp.zeros_like(st_ref)

    xbc = jnp.concatenate([x_ref[0], bc_ref[0]], axis=1).astype(F32)
    xbc = _silu(_causal_conv(xbc, tail_ref, cw_ref[...]) + cb_ref[...])

    dt = _softplus(dt_ref[0] + par_ref[1:2, :])
    a = -jnp.exp(par_ref[0:1, :])
    acum = _cumsum_rows(dt * a)
    acum_t = jnp.concatenate([acum, jnp.zeros((LANES - c, LANES), F32)], axis=0).T if c < LANES \
        else acum.T
    ea = jnp.exp(acum)
    alast = acum[c - 1:c, :]
    ealast = jnp.exp(alast)
    wdec = jnp.exp(alast - acum)
    dskip = par_ref[2:3, :]
    incl, _ = _tri_masks(c)
    gw = hpg * p_dim
    head_of_lane = lax.broadcasted_iota(jnp.int32, (1, gw), 1) // p_dim
    groups = range(SSD_GROUPS)

    def expand(a, g):
        out = a[:, g * hpg:g * hpg + 1]
        for j in range(1, hpg):
            out = jnp.where(head_of_lane == j, a[:, g * hpg + j:g * hpg + j + 1], out)
        return jnp.broadcast_to(out, (a.shape[0], gw))

    bms = [xbc[:, d_inner + g * gsz:d_inner + (g + 1) * gsz] for g in groups]
    cms = [xbc[:, d_inner + (SSD_GROUPS + g) * gsz:d_inner + (SSD_GROUPS + g + 1) * gsz]
           for g in groups]
    xgs = [xbc[:, g * gw:(g + 1) * gw] for g in groups]
    cbs = [_dot_nt(cms[g], bms[g]) for g in groups]
    xdts = [xgs[g] * expand(dt, g) for g in groups]
    sts = [st_ref[g] for g in groups]

    ys = []
    for g in groups:
        lhs = jnp.concatenate(
            [cbs[g] * _decay_matrix(acum[:, h:h + 1], acum_t[h:h + 1, :c], incl)
             for h in range(g * hpg, (g + 1) * hpg)], axis=1)
        xb = xdts[g].astype(BF16)
        rhs = jnp.concatenate([jnp.where(head_of_lane == j, xb, jnp.zeros_like(xb))
                               for j in range(hpg)], axis=0)
        y = _dot(lhs, rhs) + _dot(cms[g], sts[g]) * expand(ea, g) + expand(dskip, g) * xgs[g]
        ys.append(y)
    for g in groups:
        st_ref[g] = sts[g] * expand(ealast, g) + _dot_tn(bms[g], xdts[g] * expand(wdec, g))
    for g in groups:
        sl = slice(g * gw, (g + 1) * gw)
        y = ys[g] * _silu(z_ref[0, :, sl].astype(F32))
        ms = jnp.mean(y * y, axis=-1, keepdims=True)
        o_ref[0, :, sl] = (y * lax.rsqrt(ms + NORM_EPS) * ng_ref[:, sl]).astype(BF16)


def _ssd_scan(main, dtraw, conv_w, conv_b, a_log, dt_bias, d_skip, norm_g, heads, d_inner):
    b, s, _ = main.shape
    c = min(SSD_CHUNK, s)
    nbc = 2 * SSD_GROUPS * SSD_STATE
    assert nbc == d_inner
    nconv = d_inner + nbc
    padl = lambda v: jnp.concatenate([v, jnp.zeros((LANES - heads,), F32)])
    par = jnp.stack([padl(a_log), padl(dt_bias), padl(d_skip)])
    tile = lambda j: pl.BlockSpec((1, c, d_inner), lambda i, t: (i, t, j))
    return pl.pallas_call(
        functools.partial(_ssd_kernel, heads=heads),
        grid=(b, s // c),
        in_specs=[tile(0), tile(1), tile(2),
                  pl.BlockSpec((1, c, LANES), lambda i, t: (i, t, 0)),
                  _const_spec((CONV_K, nconv)), _const_spec((1, nconv)), _const_spec((3, LANES)),
                  _const_spec((1, d_inner))],
        out_specs=tile(0),
        out_shape=jax.ShapeDtypeStruct((b, s, d_inner), BF16),
        scratch_shapes=[pltpu.VMEM((SUBLANES, nconv), F32),
                        pltpu.VMEM((SSD_GROUPS, SSD_STATE, d_inner // SSD_GROUPS), F32)],
        compiler_params=_params(("arbitrary", "arbitrary")),
        name="ssd_scan",
    )(main, main, main, dtraw, conv_w, conv_b.reshape(1, nconv), par, norm_g.reshape(1, d_inner))


def _deinterleave_heads(w, heads, dk):
    d = w.shape[0]
    return w.reshape(d, heads, dk // 2, 2).transpose(0, 1, 3, 2).reshape(d, heads * dk)


def _pad_gate_cols(w):
    d, n = w.shape
    return jnp.concatenate([w, jnp.zeros((d, LANES - n), w.dtype)], axis=1)


def kernel(x, c, ada_w, ada_b, norm_mix_g, norm_mlp_g, mlp_w1, mlp_w2, final_norm_g, ret_w_in, ret_w_out, gdn_w_in, gdn_conv_w, gdn_A_log, gdn_dt_bias, gdn_norm_g, gdn_w_out, ssd_w_in, ssd_conv_w, ssd_conv_b, ssd_A_log, ssd_dt_bias, ssd_D, ssd_norm_g, ssd_w_out):
    depth = ada_w.shape[0]
    b, s, d = x.shape
    mod = _modulation(c, ada_w, ada_b)
    rope_cos, rope_sin = _rope_table(s, d // RET_HEADS // 2)

    for l in range(depth):
        sh1, sc1, gt1, sh2, sc2, gt2 = [mod[l, :, None, i * d:(i + 1) * d] for i in range(6)]
        kind = l % N_MIXERS
        j = l // N_MIXERS
        if kind == 0:
            w_in = ret_w_in[j]
            dk = d // RET_HEADS
            dv = 2 * d // RET_HEADS
            nqk = RET_HEADS * dk
            w_in = jnp.concatenate([_deinterleave_heads(w_in[:, :nqk], RET_HEADS, dk),
                                    _deinterleave_heads(w_in[:, nqk:2 * nqk], RET_HEADS, dk),
                                    w_in[:, 2 * nqk:]], axis=1).astype(BF16)
            proj, _ = _inproj(x, sh1, sc1, norm_mix_g[l], w_in)
            o = _retention_scan(proj, rope_cos, rope_sin, RET_HEADS, dk, dv)
            w_out = ret_w_out[j]
        elif kind == 1:
            w_in = gdn_w_in[j]
            nmain = 2 * GDN_QK_HEADS * GDN_DK + 2 * GDN_V_HEADS * GDN_DV
            proj, gates = _inproj(x, sh1, sc1, norm_mix_g[l], w_in[:, :nmain].astype(BF16),
                                  _pad_gate_cols(w_in[:, nmain:]).astype(BF16))
            o = _gdn_scan(proj, gates, gdn_conv_w[j], gdn_A_log[j], gdn_dt_bias[j], gdn_norm_g[j])
            w_out = gdn_w_out[j]
        else:
            w_in = ssd_w_in[j]
            d_inner = ssd_w_out.shape[1]
            heads = ssd_A_log.shape[1]
            nmain = 2 * d_inner + 2 * SSD_GROUPS * SSD_STATE
            proj, dtraw = _inproj(x, sh1, sc1, norm_mix_g[l], w_in[:, :nmain].astype(BF16),
                                  _pad_gate_cols(w_in[:, nmain:]).astype(BF16))
            o = _ssd_scan(proj, dtraw, ssd_conv_w[j], ssd_conv_b[j], ssd_A_log[j], ssd_dt_bias[j],
                          ssd_D[j], ssd_norm_g[j], heads, d_inner)
            w_out = ssd_w_out[j]
        x = _outmlp(o, x, gt1, sh2, sc2, gt2, norm_mlp_g[l], w_out.astype(BF16),
                    mlp_w1[l].astype(BF16), mlp_w2[l].astype(BF16),
                    gf=final_norm_g if l == depth - 1 else None)
    return x
```

```python
import functools
import math

import numpy as np
import jax
import jax.numpy as jnp
from jax import lax
from jax.experimental import pallas as pl
from jax.experimental.pallas import tpu as pltpu

F32 = jnp.float32
BF16 = jnp.bfloat16

NORM_EPS = 1e-6
CONV_K = 4
N_MIXERS = 3

RET_HEADS = 4
RET_ROPE_BASE = 10000.0
RET_CHUNK = 256

GDN_QK_HEADS = 8
GDN_V_HEADS = 16
GDN_DK = 128
GDN_DV = 128
GDN_CHUNK = 64
GDN_SOLVE_BLOCK = 16
SEQS_PER_STEP = 2

SSD_HEAD_DIM = 64
SSD_GROUPS = 8
SSD_STATE = 128
SSD_CHUNK = 128

LANES = 128
SUBLANES = 8
VMEM_LIMIT_BYTES = 56 * 1024 * 1024

NT_DIMS = (((1,), (1,)), ((), ()))
TN_DIMS = (((0,), (0,)), ((), ()))


def _dot(a, b):
    return jnp.dot(a.astype(BF16), b.astype(BF16), preferred_element_type=F32)


def _dot_nt(a, b):
    return lax.dot_general(a.astype(BF16), b.astype(BF16), NT_DIMS, preferred_element_type=F32)


def _dot_tn(a, b):
    return lax.dot_general(a.astype(BF16), b.astype(BF16), TN_DIMS, preferred_element_type=F32)


def _silu(x):
    return x * jax.nn.sigmoid(x)


def _softplus(x):
    return jnp.maximum(x, 0.0) + jnp.log1p(jnp.exp(-jnp.abs(x)))


def _params(sem):
    return pltpu.CompilerParams(dimension_semantics=sem, vmem_limit_bytes=VMEM_LIMIT_BYTES)


def _const_spec(shape):
    nd = len(shape)
    return pl.BlockSpec(shape, lambda *_: (0,) * nd, pipeline_mode=pl.Buffered(1))


def _mod_kernel(c_ref, w_ref, b_ref, o_ref):
    cond = _silu(c_ref[...])
    o_ref[0] = jnp.dot(cond, w_ref[0], precision=lax.Precision.HIGHEST,
                       preferred_element_type=F32) + b_ref[0]


def _modulation(c, ada_w, ada_b):
    depth, d, d6 = ada_w.shape
    b = c.shape[0]
    return pl.pallas_call(
        _mod_kernel,
        grid=(depth, d6 // d),
        in_specs=[pl.BlockSpec((b, d), lambda l, j: (0, 0)),
                  pl.BlockSpec((1, d, d), lambda l, j: (l, 0, j)),
                  pl.BlockSpec((1, 1, d), lambda l, j: (l, 0, j))],
        out_specs=pl.BlockSpec((1, b, d), lambda l, j: (l, 0, j)),
        out_shape=jax.ShapeDtypeStruct((depth, b, d6), F32),
        compiler_params=_params(("arbitrary", "arbitrary")),
        name="adaln_mod",
    )(c, ada_w, ada_b.reshape(depth, 1, d6))


def _norm_mod(x, gamma, scale, shift):
    ms = jnp.mean(x * x, axis=-1, keepdims=True)
    return x * lax.rsqrt(ms + NORM_EPS) * gamma * (1.0 + scale) + shift


def _inproj_kernel(x_ref, sh_ref, sc_ref, g_ref, w_ref, *rest, n_chunk, has_gate):
    if has_gate:
        wg_ref, o_ref, og_ref = rest
    else:
        (o_ref,) = rest
    h = _norm_mod(x_ref[0], g_ref[...], sc_ref[0], sh_ref[0]).astype(BF16)
    n = w_ref.shape[1]
    for j in range(n // n_chunk):
        sl = slice(j * n_chunk, (j + 1) * n_chunk)
        o_ref[0, :, sl] = jnp.dot(h, w_ref[:, sl], preferred_element_type=F32).astype(BF16)
    if has_gate:
        og_ref[0] = jnp.dot(h, wg_ref[...], preferred_element_type=F32)


def _inproj(x, shift, scale, gamma, w, wg=None, tm=1024, n_chunk=512):
    b, s, d = x.shape
    n = w.shape[1]
    tm = min(tm, s)
    has_gate = wg is not None
    row = pl.BlockSpec((1, 1, d), lambda i, t: (i, 0, 0))
    in_specs = [pl.BlockSpec((1, tm, d), lambda i, t: (i, t, 0)), row, row,
                _const_spec((1, d)), _const_spec((d, n))]
    args = [x, shift, scale, gamma.reshape(1, d), w]
    out_specs = [pl.BlockSpec((1, tm, n), lambda i, t: (i, t, 0))]
    out_shape = [jax.ShapeDtypeStruct((b, s, n), BF16)]
    if has_gate:
        in_specs.append(_const_spec((d, LANES)))
        args.append(wg)
        out_specs.append(pl.BlockSpec((1, tm, LANES), lambda i, t: (i, t, 0)))
        out_shape.append(jax.ShapeDtypeStruct((b, s, LANES), F32))
    out = pl.pallas_call(
        functools.partial(_inproj_kernel, n_chunk=n_chunk, has_gate=has_gate),
        grid=(b, s // tm),
        in_specs=in_specs, out_specs=out_specs, out_shape=out_shape,
        compiler_params=_params(("arbitrary", "arbitrary")),
        name="norm_inproj",
    )(*args)
    return out if has_gate else (out[0], None)


def _outmlp_kernel(o_ref, x_ref, gt1_ref, sh2_ref, sc2_ref, gt2_ref, g2_ref, wo_ref, w1_ref,
                   w2_ref, *rest, ff_chunk, final):
    if final:
        gf_ref, y_ref = rest
    else:
        (y_ref,) = rest
    y = jnp.dot(o_ref[0], wo_ref[...], preferred_element_type=F32)
    x1 = x_ref[0] + (1.0 + gt1_ref[0]) * y
    h2 = _norm_mod(x1, g2_ref[...], sc2_ref[0], sh2_ref[0]).astype(BF16)
    d_ff = w1_ref.shape[1]
    m = jnp.zeros_like(x1)
    for j in range(d_ff // ff_chunk):
        sl = slice(j * ff_chunk, (j + 1) * ff_chunk)
        u = jnp.maximum(jnp.dot(h2, w1_ref[:, sl], preferred_element_type=F32), 0.0)
        m = m + jnp.dot((u * u).astype(BF16), w2_ref[sl, :], preferred_element_type=F32)
    x2 = x1 + (1.0 + gt2_ref[0]) * m
    if final:
        ms = jnp.mean(x2 * x2, axis=-1, keepdims=True)
        x2 = x2 * lax.rsqrt(ms + NORM_EPS) * gf_ref[...]
    y_ref[0] = x2


def _outmlp(o, x, gt1, sh2, sc2, gt2, g2, w_out, w1, w2, gf=None, tm=512, ff_chunk=1024):
    b, s, d = x.shape
    dv = o.shape[-1]
    d_ff = w1.shape[1]
    tm = min(tm, s)
    final = gf is not None
    row = pl.BlockSpec((1, 1, d), lambda i, t: (i, 0, 0))
    tile = pl.BlockSpec((1, tm, d), lambda i, t: (i, t, 0))
    in_specs = [pl.BlockSpec((1, tm, dv), lambda i, t: (i, t, 0)), tile, row, row, row, row,
                _const_spec((1, d)), _const_spec((dv, d)), _const_spec((d, d_ff)),
                _const_spec((d_ff, d))]
    args = [o, x, gt1, sh2, sc2, gt2, g2.reshape(1, d), w_out, w1, w2]
    if final:
        in_specs.append(_const_spec((1, d)))
        args.append(gf.reshape(1, d))
    return pl.pallas_call(
        functools.partial(_outmlp_kernel, ff_chunk=ff_chunk, final=final),
        grid=(b, s // tm),
        in_specs=in_specs, out_specs=tile,
        out_shape=jax.ShapeDtypeStruct((b, s, d), F32),
        compiler_params=_params(("arbitrary", "arbitrary")),
        name="outproj_mlp",
    )(*args)


def _causal_conv(x, tail_ref, w):
    c = x.shape[0]
    xe = jnp.concatenate([tail_ref[...], x], axis=0)
    tail_ref[...] = x[c - SUBLANES:, :]
    acc = x * w[CONV_K - 1:CONV_K, :]
    for k in range(1, CONV_K):
        acc = acc + pltpu.roll(xe, k, 0)[SUBLANES:, :] * w[CONV_K - 1 - k:CONV_K - k, :]
    return acc


def _cumsum_rows(x):
    c = x.shape[0]
    row = lax.broadcasted_iota(jnp.int32, x.shape, 0)
    sh = 1
    while sh < c:
        x = x + jnp.where(row >= sh, pltpu.roll(x, sh, 0), 0.0)
        sh *= 2
    return x


def _tri_masks(c):
    t = lax.broadcasted_iota(jnp.int32, (c, c), 0)
    s = lax.broadcasted_iota(jnp.int32, (c, c), 1)
    return t >= s, t > s


def _decay_matrix(col, rowv, incl):
    return jnp.where(incl, jnp.exp(jnp.where(incl, col - rowv, 0.0)), 0.0)


def _rope_table_kernel(invf_ref, cos_ref, sin_ref):
    rows = cos_ref.shape[0]
    pos = (pl.program_id(0) * rows + lax.broadcasted_iota(jnp.int32, cos_ref.shape, 0)).astype(F32)
    ang = pos * invf_ref[...]
    cos_ref[...] = jnp.cos(ang)
    sin_ref[...] = jnp.sin(ang)


def _rope_table(s, half, rows=1024):
    rows = min(rows, s)
    inv_freq = RET_ROPE_BASE ** (-jnp.linspace(0.0, 1.0, half, dtype=F32))
    spec = pl.BlockSpec((rows, half), lambda t: (t, 0))
    return pl.pallas_call(
        _rope_table_kernel,
        grid=(s // rows,),
        in_specs=[_const_spec((1, half))],
        out_specs=[spec, spec],
        out_shape=[jax.ShapeDtypeStruct((s, half), F32)] * 2,
        compiler_params=_params(("arbitrary",)),
        name="rope_table",
    )(inv_freq.reshape(1, half))


def _ret_kernel(q_ref, k_ref, v_ref, g_ref, cos_ref, sin_ref, dmask_ref, dvec_ref, o_ref, r_ref, *,
                heads):
    dk = q_ref.shape[2] // heads
    dv = v_ref.shape[2] // heads
    half = dk // 2
    step = pl.program_id(1)

    @pl.when(step == 0)
    def _():
        r_ref[...] = jnp.zeros_like(r_ref)

    cs = cos_ref[...]
    sn = sin_ref[...]
    dvec = dvec_ref[...]

    def rope(ref, h):
        a = ref[0, :, h * dk:h * dk + half].astype(F32)
        b = ref[0, :, h * dk + half:(h + 1) * dk].astype(F32)
        return jnp.concatenate([a * cs - b * sn, a * sn + b * cs], axis=1)

    hs = range(heads)
    qfs = [rope(q_ref, h) for h in hs]
    khs = [rope(k_ref, h) * (dk ** -0.5) for h in hs]
    vhs = [v_ref[0, :, h * dv:(h + 1) * dv] for h in hs]
    scs = [_dot_nt(qfs[h], khs[h]) * dmask_ref[h] for h in hs]
    rs = [r_ref[h] for h in hs]
    os_ = [_dot(jnp.concatenate([scs[h].astype(BF16), (qfs[h] * dvec[:, h:h + 1]).astype(BF16)],
                                axis=1),
                jnp.concatenate([vhs[h], rs[h].astype(BF16)], axis=0)) for h in hs]
    for h in hs:
        zeta = dvec[:, heads + h:heads + h + 1]
        dc = dvec[0:1, 2 * heads + h:2 * heads + h + 1]
        r_ref[h] = rs[h] * dc + _dot_tn(khs[h] * zeta, vhs[h])
    for h in hs:
        o = os_[h]
        ms = jnp.mean(o * o, axis=-1, keepdims=True)
        on = o * lax.rsqrt(ms + NORM_EPS)
        gate = _silu(g_ref[0, :, h * dv:(h + 1) * dv].astype(F32))
        o_ref[0, :, h * dv:(h + 1) * dv] = (gate * on).astype(BF16)


def _retention_scan(qkvg, rope_cos, rope_sin, heads, dk, dv):
    b, s, _ = qkvg.shape
    c = min(RET_CHUNK, s)
    nqk = heads * dk
    nv = heads * dv
    lg = jnp.log1p(-jnp.exp2(-5.0 - jnp.arange(heads, dtype=F32)))
    pos = jnp.arange(c, dtype=F32)
    rel = pos[:, None] - pos[None, :]
    causal = rel >= 0
    dmask = jnp.where(causal, jnp.exp(jnp.where(causal, rel, 0.0) * lg[:, None, None]), 0.0)
    xi = jnp.exp((pos[:, None] + 1.0) * lg[None, :])
    zeta = jnp.exp((c - 1.0 - pos)[:, None] * lg[None, :])
    dcs = jnp.broadcast_to(jnp.exp(c * lg)[None, :], (c, heads))
    dvec = jnp.concatenate([xi, zeta, dcs, jnp.zeros((c, LANES - 3 * heads), F32)], axis=1)
    assert nv == 2 * nqk
    rope_spec = pl.BlockSpec((c, dk // 2), lambda i, t: (t, 0))
    return pl.pallas_call(
        functools.partial(_ret_kernel, heads=heads),
        grid=(b, s // c),
        in_specs=[pl.BlockSpec((1, c, nqk), lambda i, t: (i, t, 0)),
                  pl.BlockSpec((1, c, nqk), lambda i, t: (i, t, 1)),
                  pl.BlockSpec((1, c, nv), lambda i, t: (i, t, 1)),
                  pl.BlockSpec((1, c, nv), lambda i, t: (i, t, 2)),
                  rope_spec, rope_spec, _const_spec((heads, c, c)), _const_spec((c, LANES))],
        out_specs=pl.BlockSpec((1, c, nv), lambda i, t: (i, t, 0)),
        out_shape=jax.ShapeDtypeStruct((b, s, nv), BF16),
        scratch_shapes=[pltpu.VMEM((heads, dk, dv), F32)],
        compiler_params=_params(("arbitrary", "arbitrary")),
        name="retention_scan",
    )(qkvg, qkvg, qkvg, qkvg, rope_cos, rope_sin, dmask, dvec)


def _unit_lower_solve(amats, rhs):
    c = amats[0].shape[0]
    blk = GDN_SOLVE_BLOCK
    row = lax.broadcasted_iota(jnp.int32, (c, c), 0)
    colm = lax.broadcasted_iota(jnp.int32, (c, c), 1)
    same = (row // blk) == (colm // blk)
    eye = (row == colm).astype(F32)
    ps = [jnp.where(same, -a, 0.0) for a in amats]
    offs = [jnp.where(same, 0.0, a) for a in amats]
    ts = [eye + p for p in ps]
    ps = [_dot(p, p) for p in ps]
    n_sq = int(math.log2(blk))
    for it in range(1, n_sq):
        if it + 1 < n_sq:
            tps = [_dot(jnp.concatenate([t, p], axis=0), p) for t, p in zip(ts, ps)]
            ts = [t + tp[:c] for t, tp in zip(ts, tps)]
            ps = [tp[c:] for tp in tps]
        else:
            ts = [t + _dot(t, p) for t, p in zip(ts, ps)]
    nps = [_dot(t, off) for t, off in zip(ts, offs)]
    xs = [_dot(t, r) for t, r in zip(ts, rhs)]
    for i in range(1, c // blk):
        lo, hi = i * blk, (i + 1) * blk
        xs = [jnp.concatenate([x[:lo], x[lo:hi] - _dot(n[lo:hi, :], x)] + ([x[hi:]] if hi < c else []),
                              axis=0) for n, x in zip(nps, xs)]
    return xs

def _gdn_kernel(qkv_ref, z_ref, gate_ref, cw_ref, par_ref, ng_ref, o_ref, tail_ref, st_ref):
    c = qkv_ref.shape[1]
    nqk = GDN_QK_HEADS * GDN_DK
    rep = GDN_V_HEADS // GDN_QK_HEADS
    step = pl.program_id(1)

    @pl.when(step == 0)
    def _():
        tail_ref[...] = jnp.zeros_like(tail_ref)
        st_ref[...] = jnp.zeros_like(st_ref)

    incl, strict = _tri_masks(c)
    ng = ng_ref[...]
    heads = range(GDN_V_HEADS)
    col = lambda a, h: a[:, GDN_V_HEADS + h:GDN_V_HEADS + h + 1]

    def one_sequence(bi):
        qkv = _silu(_causal_conv(qkv_ref[bi].astype(F32), tail_ref.at[bi], cw_ref[...]))

        gates = gate_ref[bi]
        beta = jax.nn.sigmoid(gates)
        g = -jnp.exp(par_ref[0:1, :]) * _softplus(gates + par_ref[1:2, :])
        gc = _cumsum_rows(g)
        gct = jnp.concatenate([gc, jnp.zeros((LANES - c, LANES), F32)], axis=0).T
        eg = jnp.exp(gc)
        glast = gc[c - 1:c, :]
        eglast = jnp.exp(glast)
        kdec = jnp.exp(glast - gc)

        qs, ks, qkk = [], [], []
        for j in range(GDN_QK_HEADS):
            q = qkv[:, j * GDN_DK:(j + 1) * GDN_DK]
            k = qkv[:, nqk + j * GDN_DK:nqk + (j + 1) * GDN_DK]
            q = q * lax.rsqrt(jnp.sum(q * q, axis=-1, keepdims=True) + 1e-6) * (GDN_DK ** -0.5)
            k = k * lax.rsqrt(jnp.sum(k * k, axis=-1, keepdims=True) + 1e-6)
            qs.append(q)
            ks.append(k)
            qkk.append(_dot_nt(jnp.concatenate([q, k], axis=0), k))

        decays, amats, xs = [], [], []
        for h in heads:
            j = h // rep
            v = qkv[:, 2 * nqk + h * GDN_DV:2 * nqk + (h + 1) * GDN_DV]
            bcol = beta[:, h:h + 1]
            grow = gct[GDN_V_HEADS + h:GDN_V_HEADS + h + 1, :c]
            decay = _decay_matrix(col(gc, h), grow, incl)
            decays.append(decay)
            amats.append(jnp.where(strict, bcol * qkk[j][c:, :] * decay, 0.0))
            xs.append(jnp.concatenate([v * bcol, ks[j] * (bcol * col(eg, h))], axis=1))
        xs = _unit_lower_solve(amats, xs)

        sts = [st_ref[bi * GDN_V_HEADS + h] for h in heads]
        v_news = [xs[h][:, :GDN_DV] - _dot(xs[h][:, GDN_DV:], sts[h]) for h in heads]
        outs = []
        for h in heads:
            j = h // rep
            lhs = jnp.concatenate([qs[j] * col(eg, h), qkk[j][:c, :] * decays[h]], axis=1)
            outs.append(_dot(lhs, jnp.concatenate([sts[h], v_news[h]], axis=0)))
        for h in heads:
            st_ref[bi * GDN_V_HEADS + h] = sts[h] * col(eglast, h) + _dot_tn(
                ks[h // rep] * col(kdec, h), v_news[h])
        for h in heads:
            o = outs[h]
            ms = jnp.mean(o * o, axis=-1, keepdims=True)
            on = o * lax.rsqrt(ms + NORM_EPS) * ng
            zg = _silu(z_ref[bi, :, h * GDN_DV:(h + 1) * GDN_DV].astype(F32))
            o_ref[bi, :, h * GDN_DV:(h + 1) * GDN_DV] = (on * zg).astype(BF16)

    for bi in range(qkv_ref.shape[0]):
        one_sequence(bi)


def _gdn_scan(main, gates, conv_w, a_log, dt_bias, norm_g):
    b, s, _ = main.shape
    c = GDN_CHUNK
    nconv = 2 * GDN_QK_HEADS * GDN_DK + GDN_V_HEADS * GDN_DV
    nv = GDN_V_HEADS * GDN_DV
    assert nconv == 2 * nv
    pad = jnp.zeros((GDN_V_HEADS,), F32)
    tail = jnp.zeros((LANES - 2 * GDN_V_HEADS,), F32)
    par = jnp.stack([jnp.concatenate([pad, a_log, tail]), jnp.concatenate([pad, dt_bias, tail])])
    nb = SEQS_PER_STEP if b % SEQS_PER_STEP == 0 else 1
    return pl.pallas_call(
        _gdn_kernel,
        grid=(b // nb, s // c),
        in_specs=[pl.BlockSpec((nb, c, nconv), lambda i, t: (i, t, 0)),
                  pl.BlockSpec((nb, c, nv), lambda i, t: (i, t, 2)),
                  pl.BlockSpec((nb, c, LANES), lambda i, t: (i, t, 0)),
                  _const_spec((CONV_K, nconv)), _const_spec((2, LANES)), _const_spec((1, GDN_DV))],
        out_specs=pl.BlockSpec((nb, c, nv), lambda i, t: (i, t, 0)),
        out_shape=jax.ShapeDtypeStruct((b, s, nv), BF16),
        scratch_shapes=[pltpu.VMEM((nb, SUBLANES, nconv), F32),
                        pltpu.VMEM((nb * GDN_V_HEADS, GDN_DK, GDN_DV), F32)],
        compiler_params=_params(("arbitrary", "arbitrary")),
        name="gdn_scan",
    )(main, main, gates, conv_w, par, norm_g.reshape(1, GDN_DV))


def _ssd_kernel(z_ref, x_ref, bc_ref, dt_ref, cw_ref, cb_ref, par_ref, ng_ref, o_ref, tail_ref,
                st_ref, *, heads):
    c = x_ref.shape[1]
    d_inner = x_ref.shape[2]
    p_dim = d_inner // heads
    gsz = SSD_STATE
    hpg = heads // SSD_GROUPS
    step = pl.program_id(1)

    @pl.when(step == 0)
    def _():
        tail_ref[...] = jnp.zeros_like(tail_ref)
        st_ref[...] = jnp.zeros_like(st_ref)

    a = -jnp.exp(par_ref[0:1, :])
    dskip = par_ref[2:3, :]
    incl, _ = _tri_masks(c)
    gw = hpg * p_dim
    head_of_lane = lax.broadcasted_iota(jnp.int32, (1, gw), 1) // p_dim
    groups = range(SSD_GROUPS)

    def expand(a, g):
        out = a[:, g * hpg:g * hpg + 1]
        for j in range(1, hpg):
            out = jnp.where(head_of_lane == j, a[:, g * hpg + j:g * hpg + j + 1], out)
        return jnp.broadcast_to(out, (a.shape[0], gw))

    def one_sequence(bi):
        xbc = jnp.concatenate([x_ref[bi], bc_ref[bi]], axis=1).astype(F32)
        xbc = _silu(_causal_conv(xbc, tail_ref.at[bi], cw_ref[...]) + cb_ref[...])

        dt = _softplus(dt_ref[bi] + par_ref[1:2, :])
        acum = _cumsum_rows(dt * a)
        acum_t = jnp.concatenate([acum, jnp.zeros((LANES - c, LANES), F32)], axis=0).T \
            if c < LANES else acum.T
        ea = jnp.exp(acum)
        alast = acum[c - 1:c, :]
        ealast = jnp.exp(alast)
        wdec = jnp.exp(alast - acum)

        bms = [xbc[:, d_inner + g * gsz:d_inner + (g + 1) * gsz] for g in groups]
        cms = [xbc[:, d_inner + (SSD_GROUPS + g) * gsz:d_inner + (SSD_GROUPS + g + 1) * gsz]
               for g in groups]
        xgs = [xbc[:, g * gw:(g + 1) * gw] for g in groups]
        cbs = [_dot_nt(cms[g], bms[g]) for g in groups]
        xdts = [xgs[g] * expand(dt, g) for g in groups]
        sts = [st_ref[bi * SSD_GROUPS + g] for g in groups]

        ys = []
        for g in groups:
            lhs = jnp.concatenate(
                [cbs[g] * _decay_matrix(acum[:, h:h + 1], acum_t[h:h + 1, :c], incl)
                 for h in range(g * hpg, (g + 1) * hpg)], axis=1)
            xb = xdts[g].astype(BF16)
            rhs = jnp.concatenate([jnp.where(head_of_lane == j, xb, jnp.zeros_like(xb))
                                   for j in range(hpg)], axis=0)
            y = _dot(lhs, rhs) + _dot(cms[g], sts[g]) * expand(ea, g) + expand(dskip, g) * xgs[g]
            ys.append(y)
        for g in groups:
            st_ref[bi * SSD_GROUPS + g] = sts[g] * expand(ealast, g) + _dot_tn(
                bms[g], xdts[g] * expand(wdec, g))
        for g in groups:
            sl = slice(g * gw, (g + 1) * gw)
            y = ys[g] * _silu(z_ref[bi, :, sl].astype(F32))
            ms = jnp.mean(y * y, axis=-1, keepdims=True)
            o_ref[bi, :, sl] = (y * lax.rsqrt(ms + NORM_EPS) * ng_ref[:, sl]).astype(BF16)

    for bi in range(x_ref.shape[0]):
        one_sequence(bi)


def _ssd_scan(main, dtraw, conv_w, conv_b, a_log, dt_bias, d_skip, norm_g, heads, d_inner):
    b, s, _ = main.shape
    c = min(SSD_CHUNK, s)
    nbc = 2 * SSD_GROUPS * SSD_STATE
    assert nbc == d_inner
    nconv = d_inner + nbc
    padl = lambda v: jnp.concatenate([v, jnp.zeros((LANES - heads,), F32)])
    par = jnp.stack([padl(a_log), padl(dt_bias), padl(d_skip)])
    nb = 1
    tile = lambda j: pl.BlockSpec((nb, c, d_inner), lambda i, t: (i, t, j))
    return pl.pallas_call(
        functools.partial(_ssd_kernel, heads=heads),
        grid=(b // nb, s // c),
        in_specs=[tile(0), tile(1), tile(2),
                  pl.BlockSpec((nb, c, LANES), lambda i, t: (i, t, 0)),
                  _const_spec((CONV_K, nconv)), _const_spec((1, nconv)), _const_spec((3, LANES)),
                  _const_spec((1, d_inner))],
        out_specs=tile(0),
        out_shape=jax.ShapeDtypeStruct((b, s, d_inner), BF16),
        scratch_shapes=[pltpu.VMEM((nb, SUBLANES, nconv), F32),
                        pltpu.VMEM((nb * SSD_GROUPS, SSD_STATE, d_inner // SSD_GROUPS), F32)],
        compiler_params=_params(("arbitrary", "arbitrary")),
        name="ssd_scan",
    )(main, main, main, dtraw, conv_w, conv_b.reshape(1, nconv), par, norm_g.reshape(1, d_inner))


def _deinterleave_heads(w, heads, dk):
    d = w.shape[0]
    return w.reshape(d, heads, dk // 2, 2).transpose(0, 1, 3, 2).reshape(d, heads * dk)


def _pad_gate_cols(w):
    d, n = w.shape
    return jnp.concatenate([w, jnp.zeros((d, LANES - n), w.dtype)], axis=1)


def kernel(x, c, ada_w, ada_b, norm_mix_g, norm_mlp_g, mlp_w1, mlp_w2, final_norm_g, ret_w_in, ret_w_out, gdn_w_in, gdn_conv_w, gdn_A_log, gdn_dt_bias, gdn_norm_g, gdn_w_out, ssd_w_in, ssd_conv_w, ssd_conv_b, ssd_A_log, ssd_dt_bias, ssd_D, ssd_norm_g, ssd_w_out):
    depth = ada_w.shape[0]
    b, s, d = x.shape
    mod = _modulation(c, ada_w, ada_b)
    rope_cos, rope_sin = _rope_table(s, d // RET_HEADS // 2)

    for l in range(depth):
        sh1, sc1, gt1, sh2, sc2, gt2 = [mod[l, :, None, i * d:(i + 1) * d] for i in range(6)]
        kind = l % N_MIXERS
        j = l // N_MIXERS
        if kind == 0:
            w_in = ret_w_in[j]
            dk = d // RET_HEADS
            dv = 2 * d // RET_HEADS
            nqk = RET_HEADS * dk
            w_in = jnp.concatenate([_deinterleave_heads(w_in[:, :nqk], RET_HEADS, dk),
                                    _deinterleave_heads(w_in[:, nqk:2 * nqk], RET_HEADS, dk),
                                    w_in[:, 2 * nqk:]], axis=1).astype(BF16)
            proj, _ = _inproj(x, sh1, sc1, norm_mix_g[l], w_in)
            o = _retention_scan(proj, rope_cos, rope_sin, RET_HEADS, dk, dv)
            w_out = ret_w_out[j]
        elif kind == 1:
            w_in = gdn_w_in[j]
            nmain = 2 * GDN_QK_HEADS * GDN_DK + 2 * GDN_V_HEADS * GDN_DV
            proj, gates = _inproj(x, sh1, sc1, norm_mix_g[l], w_in[:, :nmain].astype(BF16),
                                  _pad_gate_cols(w_in[:, nmain:]).astype(BF16))
            o = _gdn_scan(proj, gates, gdn_conv_w[j], gdn_A_log[j], gdn_dt_bias[j], gdn_norm_g[j])
            w_out = gdn_w_out[j]
        else:
            w_in = ssd_w_in[j]
            d_inner = ssd_w_out.shape[1]
            heads = ssd_A_log.shape[1]
            nmain = 2 * d_inner + 2 * SSD_GROUPS * SSD_STATE
            proj, dtraw = _inproj(x, sh1, sc1, norm_mix_g[l], w_in[:, :nmain].astype(BF16),
                                  _pad_gate_cols(w_in[:, nmain:]).astype(BF16))
            o = _ssd_scan(proj, dtraw, ssd_conv_w[j], ssd_conv_b[j], ssd_A_log[j], ssd_dt_bias[j],
                          ssd_D[j], ssd_norm_g[j], heads, d_inner)
            w_out = ssd_w_out[j]
        x = _outmlp(o, x, gt1, sh2, sc2, gt2, norm_mlp_g[l], w_out.astype(BF16),
                    mlp_w1[l].astype(BF16), mlp_w2[l].astype(BF16),
                    gf=final_norm_g if l == depth - 1 else None)
    return x
```

```python
import functools
import math

import numpy as np
import jax
import jax.numpy as jnp
from jax import lax
from jax.experimental import pallas as pl
from jax.experimental.pallas import tpu as pltpu

F32 = jnp.float32
BF16 = jnp.bfloat16

NORM_EPS = 1e-6
CONV_K = 4
N_MIXERS = 3

RET_HEADS = 4
RET_ROPE_BASE = 10000.0
RET_CHUNK = 256

GDN_QK_HEADS = 8
GDN_V_HEADS = 16
GDN_DK = 128
GDN_DV = 128
GDN_CHUNK = 64
GDN_SOLVE_BLOCK = 16
SEQS_PER_STEP = 4

SSD_HEAD_DIM = 64
SSD_GROUPS = 8
SSD_STATE = 128
SSD_CHUNK = 128

LANES = 128
SUBLANES = 8
VMEM_LIMIT_BYTES = 56 * 1024 * 1024

NT_DIMS = (((1,), (1,)), ((), ()))
TN_DIMS = (((0,), (0,)), ((), ()))


def _dot(a, b):
    return jnp.dot(a.astype(BF16), b.astype(BF16), preferred_element_type=F32)


def _dot_nt(a, b):
    return lax.dot_general(a.astype(BF16), b.astype(BF16), NT_DIMS, preferred_element_type=F32)


def _dot_tn(a, b):
    return lax.dot_general(a.astype(BF16), b.astype(BF16), TN_DIMS, preferred_element_type=F32)


def _silu(x):
    return x * jax.nn.sigmoid(x)


def _softplus(x):
    return jnp.maximum(x, 0.0) + jnp.log1p(jnp.exp(-jnp.abs(x)))


def _params(sem):
    return pltpu.CompilerParams(dimension_semantics=sem, vmem_limit_bytes=VMEM_LIMIT_BYTES)


def _const_spec(shape):
    nd = len(shape)
    return pl.BlockSpec(shape, lambda *_: (0,) * nd, pipeline_mode=pl.Buffered(1))


def _mod_kernel(c_ref, w_ref, b_ref, o_ref):
    cond = _silu(c_ref[...])
    o_ref[0] = jnp.dot(cond, w_ref[0], precision=lax.Precision.HIGHEST,
                       preferred_element_type=F32) + b_ref[0]


def _modulation(c, ada_w, ada_b):
    depth, d, d6 = ada_w.shape
    b = c.shape[0]
    return pl.pallas_call(
        _mod_kernel,
        grid=(depth, d6 // d),
        in_specs=[pl.BlockSpec((b, d), lambda l, j: (0, 0)),
                  pl.BlockSpec((1, d, d), lambda l, j: (l, 0, j)),
                  pl.BlockSpec((1, 1, d), lambda l, j: (l, 0, j))],
        out_specs=pl.BlockSpec((1, b, d), lambda l, j: (l, 0, j)),
        out_shape=jax.ShapeDtypeStruct((depth, b, d6), F32),
        compiler_params=_params(("arbitrary", "arbitrary")),
        name="adaln_mod",
    )(c, ada_w, ada_b.reshape(depth, 1, d6))


def _norm_mod(x, gamma, scale, shift):
    ms = jnp.mean(x * x, axis=-1, keepdims=True)
    return x * lax.rsqrt(ms + NORM_EPS) * gamma * (1.0 + scale) + shift


def _inproj_kernel(x_ref, sh_ref, sc_ref, g_ref, w_ref, *rest, n_chunk, has_gate):
    if has_gate:
        wg_ref, o_ref, og_ref = rest
    else:
        (o_ref,) = rest
    h = _norm_mod(x_ref[0], g_ref[...], sc_ref[0], sh_ref[0]).astype(BF16)
    n = w_ref.shape[1]
    for j in range(n // n_chunk):
        sl = slice(j * n_chunk, (j + 1) * n_chunk)
        o_ref[0, :, sl] = jnp.dot(h, w_ref[:, sl], preferred_element_type=F32).astype(BF16)
    if has_gate:
        og_ref[0] = jnp.dot(h, wg_ref[...], preferred_element_type=F32)


def _inproj(x, shift, scale, gamma, w, wg=None, tm=1024, n_chunk=512):
    b, s, d = x.shape
    n = w.shape[1]
    tm = min(tm, s)
    has_gate = wg is not None
    row = pl.BlockSpec((1, 1, d), lambda i, t: (i, 0, 0))
    in_specs = [pl.BlockSpec((1, tm, d), lambda i, t: (i, t, 0)), row, row,
                _const_spec((1, d)), _const_spec((d, n))]
    args = [x, shift, scale, gamma.reshape(1, d), w]
    out_specs = [pl.BlockSpec((1, tm, n), lambda i, t: (i, t, 0))]
    out_shape = [jax.ShapeDtypeStruct((b, s, n), BF16)]
    if has_gate:
        in_specs.append(_const_spec((d, LANES)))
        args.append(wg)
        out_specs.append(pl.BlockSpec((1, tm, LANES), lambda i, t: (i, t, 0)))
        out_shape.append(jax.ShapeDtypeStruct((b, s, LANES), F32))
    out = pl.pallas_call(
        functools.partial(_inproj_kernel, n_chunk=n_chunk, has_gate=has_gate),
        grid=(b, s // tm),
        in_specs=in_specs, out_specs=out_specs, out_shape=out_shape,
        compiler_params=_params(("arbitrary", "arbitrary")),
        name="norm_inproj",
    )(*args)
    return out if has_gate else (out[0], None)


def _outmlp_kernel(o_ref, x_ref, gt1_ref, sh2_ref, sc2_ref, gt2_ref, g2_ref, wo_ref, w1_ref,
                   w2_ref, *rest, ff_chunk, final):
    if final:
        gf_ref, y_ref = rest
    else:
        (y_ref,) = rest
    y = jnp.dot(o_ref[0], wo_ref[...], preferred_element_type=F32)
    x1 = x_ref[0] + (1.0 + gt1_ref[0]) * y
    h2 = _norm_mod(x1, g2_ref[...], sc2_ref[0], sh2_ref[0]).astype(BF16)
    d_ff = w1_ref.shape[1]
    m = jnp.zeros_like(x1)
    for j in range(d_ff // ff_chunk):
        sl = slice(j * ff_chunk, (j + 1) * ff_chunk)
        u = jnp.maximum(jnp.dot(h2, w1_ref[:, sl], preferred_element_type=F32), 0.0)
        m = m + jnp.dot((u * u).astype(BF16), w2_ref[sl, :], preferred_element_type=F32)
    x2 = x1 + (1.0 + gt2_ref[0]) * m
    if final:
        ms = jnp.mean(x2 * x2, axis=-1, keepdims=True)
        x2 = x2 * lax.rsqrt(ms + NORM_EPS) * gf_ref[...]
    y_ref[0] = x2


def _outmlp(o, x, gt1, sh2, sc2, gt2, g2, w_out, w1, w2, gf=None, tm=512, ff_chunk=1024):
    b, s, d = x.shape
    dv = o.shape[-1]
    d_ff = w1.shape[1]
    tm = min(tm, s)
    final = gf is not None
    row = pl.BlockSpec((1, 1, d), lambda i, t: (i, 0, 0))
    tile = pl.BlockSpec((1, tm, d), lambda i, t: (i, t, 0))
    in_specs = [pl.BlockSpec((1, tm, dv), lambda i, t: (i, t, 0)), tile, row, row, row, row,
                _const_spec((1, d)), _const_spec((dv, d)), _const_spec((d, d_ff)),
                _const_spec((d_ff, d))]
    args = [o, x, gt1, sh2, sc2, gt2, g2.reshape(1, d), w_out, w1, w2]
    if final:
        in_specs.append(_const_spec((1, d)))
        args.append(gf.reshape(1, d))
    return pl.pallas_call(
        functools.partial(_outmlp_kernel, ff_chunk=ff_chunk, final=final),
        grid=(b, s // tm),
        in_specs=in_specs, out_specs=tile,
        out_shape=jax.ShapeDtypeStruct((b, s, d), F32),
        compiler_params=_params(("arbitrary", "arbitrary")),
        name="outproj_mlp",
    )(*args)


def _causal_conv(x, tail_ref, w):
    c = x.shape[0]
    xe = jnp.concatenate([tail_ref[...], x], axis=0)
    tail_ref[...] = x[c - SUBLANES:, :]
    acc = x * w[CONV_K - 1:CONV_K, :]
    for k in range(1, CONV_K):
        acc = acc + pltpu.roll(xe, k, 0)[SUBLANES:, :] * w[CONV_K - 1 - k:CONV_K - k, :]
    return acc


def _cumsum_rows(x):
    c = x.shape[0]
    row = lax.broadcasted_iota(jnp.int32, x.shape, 0)
    sh = 1
    while sh < c:
        x = x + jnp.where(row >= sh, pltpu.roll(x, sh, 0), 0.0)
        sh *= 2
    return x


def _tri_masks(c):
    t = lax.broadcasted_iota(jnp.int32, (c, c), 0)
    s = lax.broadcasted_iota(jnp.int32, (c, c), 1)
    return t >= s, t > s


def _decay_matrix(col, rowv, incl):
    return jnp.where(incl, jnp.exp(jnp.where(incl, col - rowv, 0.0)), 0.0)


def _rope_table_kernel(invf_ref, cos_ref, sin_ref):
    rows = cos_ref.shape[0]
    pos = (pl.program_id(0) * rows + lax.broadcasted_iota(jnp.int32, cos_ref.shape, 0)).astype(F32)
    ang = pos * invf_ref[...]
    cos_ref[...] = jnp.cos(ang)
    sin_ref[...] = jnp.sin(ang)


def _rope_table(s, half, rows=1024):
    rows = min(rows, s)
    inv_freq = RET_ROPE_BASE ** (-jnp.linspace(0.0, 1.0, half, dtype=F32))
    spec = pl.BlockSpec((rows, half), lambda t: (t, 0))
    return pl.pallas_call(
        _rope_table_kernel,
        grid=(s // rows,),
        in_specs=[_const_spec((1, half))],
        out_specs=[spec, spec],
        out_shape=[jax.ShapeDtypeStruct((s, half), F32)] * 2,
        compiler_params=_params(("arbitrary",)),
        name="rope_table",
    )(inv_freq.reshape(1, half))


def _ret_kernel(q_ref, k_ref, v_ref, g_ref, cos_ref, sin_ref, dmask_ref, dvec_ref, o_ref, r_ref, *,
                heads):
    dk = q_ref.shape[2] // heads
    dv = v_ref.shape[2] // heads
    half = dk // 2
    step = pl.program_id(1)

    @pl.when(step == 0)
    def _():
        r_ref[...] = jnp.zeros_like(r_ref)

    cs = cos_ref[...]
    sn = sin_ref[...]
    dvec = dvec_ref[...]

    def rope(ref, h):
        a = ref[0, :, h * dk:h * dk + half].astype(F32)
        b = ref[0, :, h * dk + half:(h + 1) * dk].astype(F32)
        return jnp.concatenate([a * cs - b * sn, a * sn + b * cs], axis=1)

    hs = range(heads)
    qfs = [rope(q_ref, h) for h in hs]
    khs = [rope(k_ref, h) * (dk ** -0.5) for h in hs]
    vhs = [v_ref[0, :, h * dv:(h + 1) * dv] for h in hs]
    scs = [_dot_nt(qfs[h], khs[h]) * dmask_ref[h] for h in hs]
    rs = [r_ref[h] for h in hs]
    os_ = [_dot(jnp.concatenate([scs[h].astype(BF16), (qfs[h] * dvec[:, h:h + 1]).astype(BF16)],
                                axis=1),
                jnp.concatenate([vhs[h], rs[h].astype(BF16)], axis=0)) for h in hs]
    for h in hs:
        zeta = dvec[:, heads + h:heads + h + 1]
        dc = dvec[0:1, 2 * heads + h:2 * heads + h + 1]
        r_ref[h] = rs[h] * dc + _dot_tn(khs[h] * zeta, vhs[h])
    for h in hs:
        o = os_[h]
        ms = jnp.mean(o * o, axis=-1, keepdims=True)
        on = o * lax.rsqrt(ms + NORM_EPS)
        gate = _silu(g_ref[0, :, h * dv:(h + 1) * dv].astype(F32))
        o_ref[0, :, h * dv:(h + 1) * dv] = (gate * on).astype(BF16)


def _retention_scan(qkvg, rope_cos, rope_sin, heads, dk, dv):
    b, s, _ = qkvg.shape
    c = min(RET_CHUNK, s)
    nqk = heads * dk
    nv = heads * dv
    lg = jnp.log1p(-jnp.exp2(-5.0 - jnp.arange(heads, dtype=F32)))
    pos = jnp.arange(c, dtype=F32)
    rel = pos[:, None] - pos[None, :]
    causal = rel >= 0
    dmask = jnp.where(causal, jnp.exp(jnp.where(causal, rel, 0.0) * lg[:, None, None]), 0.0)
    xi = jnp.exp((pos[:, None] + 1.0) * lg[None, :])
    zeta = jnp.exp((c - 1.0 - pos)[:, None] * lg[None, :])
    dcs = jnp.broadcast_to(jnp.exp(c * lg)[None, :], (c, heads))
    dvec = jnp.concatenate([xi, zeta, dcs, jnp.zeros((c, LANES - 3 * heads), F32)], axis=1)
    assert nv == 2 * nqk
    rope_spec = pl.BlockSpec((c, dk // 2), lambda i, t: (t, 0))
    return pl.pallas_call(
        functools.partial(_ret_kernel, heads=heads),
        grid=(b, s // c),
        in_specs=[pl.BlockSpec((1, c, nqk), lambda i, t: (i, t, 0)),
                  pl.BlockSpec((1, c, nqk), lambda i, t: (i, t, 1)),
                  pl.BlockSpec((1, c, nv), lambda i, t: (i, t, 1)),
                  pl.BlockSpec((1, c, nv), lambda i, t: (i, t, 2)),
                  rope_spec, rope_spec, _const_spec((heads, c, c)), _const_spec((c, LANES))],
        out_specs=pl.BlockSpec((1, c, nv), lambda i, t: (i, t, 0)),
        out_shape=jax.ShapeDtypeStruct((b, s, nv), BF16),
        scratch_shapes=[pltpu.VMEM((heads, dk, dv), F32)],
        compiler_params=_params(("arbitrary", "arbitrary")),
        name="retention_scan",
    )(qkvg, qkvg, qkvg, qkvg, rope_cos, rope_sin, dmask, dvec)


def _unit_lower_solve(amats, rhs):
    c = amats[0].shape[0]
    blk = GDN_SOLVE_BLOCK
    row = lax.broadcasted_iota(jnp.int32, (c, c), 0)
    colm = lax.broadcasted_iota(jnp.int32, (c, c), 1)
    same = (row // blk) == (colm // blk)
    eye = (row == colm).astype(F32)
    ps = [jnp.where(same, -a, 0.0) for a in amats]
    offs = [jnp.where(same, 0.0, a) for a in amats]
    ts = [eye + p for p in ps]
    ps = [_dot(p, p) for p in ps]
    n_sq = int(math.log2(blk))
    for it in range(1, n_sq):
        if it + 1 < n_sq:
            tps = [_dot(jnp.concatenate([t, p], axis=0), p) for t, p in zip(ts, ps)]
            ts = [t + tp[:c] for t, tp in zip(ts, tps)]
            ps = [tp[c:] for tp in tps]
        else:
            ts = [t + _dot(t, p) for t, p in zip(ts, ps)]
    nps = [_dot(t, off) for t, off in zip(ts, offs)]
    xs = [_dot(t, r) for t, r in zip(ts, rhs)]
    for i in range(1, c // blk):
        lo, hi = i * blk, (i + 1) * blk
        xs = [jnp.concatenate([x[:lo], x[lo:hi] - _dot(n[lo:hi, :], x)] + ([x[hi:]] if hi < c else []),
                              axis=0) for n, x in zip(nps, xs)]
    return xs

def _gdn_kernel(qkv_ref, z_ref, gate_ref, cw_ref, par_ref, ng_ref, o_ref, tail_ref, st_ref):
    c = qkv_ref.shape[1]
    nqk = GDN_QK_HEADS * GDN_DK
    rep = GDN_V_HEADS // GDN_QK_HEADS
    step = pl.program_id(1)

    @pl.when(step == 0)
    def _():
        tail_ref[...] = jnp.zeros_like(tail_ref)
        st_ref[...] = jnp.zeros_like(st_ref)

    incl, strict = _tri_masks(c)
    ng = ng_ref[...]
    heads = range(GDN_V_HEADS)
    col = lambda a, h: a[:, GDN_V_HEADS + h:GDN_V_HEADS + h + 1]

    def one_sequence(bi):
        qkv = _silu(_causal_conv(qkv_ref[bi].astype(F32), tail_ref.at[bi], cw_ref[...]))

        gates = gate_ref[bi]
        beta = jax.nn.sigmoid(gates)
        g = -jnp.exp(par_ref[0:1, :]) * _softplus(gates + par_ref[1:2, :])
        gc = _cumsum_rows(g)
        gct = jnp.concatenate([gc, jnp.zeros((LANES - c, LANES), F32)], axis=0).T
        eg = jnp.exp(gc)
        glast = gc[c - 1:c, :]
        eglast = jnp.exp(glast)
        kdec = jnp.exp(glast - gc)

        qs, ks, qkk = [], [], []
        for j in range(GDN_QK_HEADS):
            q = qkv[:, j * GDN_DK:(j + 1) * GDN_DK]
            k = qkv[:, nqk + j * GDN_DK:nqk + (j + 1) * GDN_DK]
            q = q * lax.rsqrt(jnp.sum(q * q, axis=-1, keepdims=True) + 1e-6) * (GDN_DK ** -0.5)
            k = k * lax.rsqrt(jnp.sum(k * k, axis=-1, keepdims=True) + 1e-6)
            qs.append(q)
            ks.append(k)
            qkk.append(_dot_nt(jnp.concatenate([q, k], axis=0), k))

        decays, amats, xs = [], [], []
        for h in heads:
            j = h // rep
            v = qkv[:, 2 * nqk + h * GDN_DV:2 * nqk + (h + 1) * GDN_DV]
            bcol = beta[:, h:h + 1]
            grow = gct[GDN_V_HEADS + h:GDN_V_HEADS + h + 1, :c]
            decay = _decay_matrix(col(gc, h), grow, incl)
            decays.append(decay)
            amats.append(jnp.where(strict, bcol * qkk[j][c:, :] * decay, 0.0))
            xs.append(jnp.concatenate([v * bcol, ks[j] * (bcol * col(eg, h))], axis=1))
        xs = _unit_lower_solve(amats, xs)

        sts = [st_ref[bi * GDN_V_HEADS + h] for h in heads]
        v_news = [xs[h][:, :GDN_DV] - _dot(xs[h][:, GDN_DV:], sts[h]) for h in heads]
        outs = []
        for h in heads:
            j = h // rep
            lhs = jnp.concatenate([qs[j] * col(eg, h), qkk[j][:c, :] * decays[h]], axis=1)
            outs.append(_dot(lhs, jnp.concatenate([sts[h], v_news[h]], axis=0)))
        for h in heads:
            st_ref[bi * GDN_V_HEADS + h] = sts[h] * col(eglast, h) + _dot_tn(
                ks[h // rep] * col(kdec, h), v_news[h])
        for h in heads:
            o = outs[h]
            ms = jnp.mean(o * o, axis=-1, keepdims=True)
            on = o * lax.rsqrt(ms + NORM_EPS) * ng
            zg = _silu(z_ref[bi, :, h * GDN_DV:(h + 1) * GDN_DV].astype(F32))
            o_ref[bi, :, h * GDN_DV:(h + 1) * GDN_DV] = (on * zg).astype(BF16)

    for bi in range(qkv_ref.shape[0]):
        one_sequence(bi)


def _gdn_scan(main, gates, conv_w, a_log, dt_bias, norm_g):
    b, s, _ = main.shape
    c = GDN_CHUNK
    nconv = 2 * GDN_QK_HEADS * GDN_DK + GDN_V_HEADS * GDN_DV
    nv = GDN_V_HEADS * GDN_DV
    assert nconv == 2 * nv
    pad = jnp.zeros((GDN_V_HEADS,), F32)
    tail = jnp.zeros((LANES - 2 * GDN_V_HEADS,), F32)
    par = jnp.stack([jnp.concatenate([pad, a_log, tail]), jnp.concatenate([pad, dt_bias, tail])])
    nb = SEQS_PER_STEP if b % SEQS_PER_STEP == 0 else 1
    return pl.pallas_call(
        _gdn_kernel,
        grid=(b // nb, s // c),
        in_specs=[pl.BlockSpec((nb, c, nconv), lambda i, t: (i, t, 0)),
                  pl.BlockSpec((nb, c, nv), lambda i, t: (i, t, 2)),
                  pl.BlockSpec((nb, c, LANES), lambda i, t: (i, t, 0)),
                  _const_spec((CONV_K, nconv)), _const_spec((2, LANES)), _const_spec((1, GDN_DV))],
        out_specs=pl.BlockSpec((nb, c, nv), lambda i, t: (i, t, 0)),
        out_shape=jax.ShapeDtypeStruct((b, s, nv), BF16),
        scratch_shapes=[pltpu.VMEM((nb, SUBLANES, nconv), F32),
                        pltpu.VMEM((nb * GDN_V_HEADS, GDN_DK, GDN_DV), F32)],
        compiler_params=_params(("arbitrary", "arbitrary")),
        name="gdn_scan",
    )(main, main, gates, conv_w, par, norm_g.reshape(1, GDN_DV))


def _ssd_kernel(z_ref, x_ref, bc_ref, dt_ref, cw_ref, cb_ref, par_ref, ng_ref, o_ref, tail_ref,
                st_ref, *, heads):
    c = x_ref.shape[1]
    d_inner = x_ref.shape[2]
    p_dim = d_inner // heads
    gsz = SSD_STATE
    hpg = heads // SSD_GROUPS
    step = pl.program_id(1)

    @pl.when(step == 0)
    def _():
        tail_ref[...] = jnp.zeros_like(tail_ref)
        st_ref[...] = jnp.zeros_like(st_ref)

    a = -jnp.exp(par_ref[0:1, :])
    dskip = par_ref[2:3, :]
    incl, _ = _tri_masks(c)
    gw = hpg * p_dim
    head_of_lane = lax.broadcasted_iota(jnp.int32, (1, gw), 1) // p_dim
    groups = range(SSD_GROUPS)

    def expand(a, g):
        out = a[:, g * hpg:g * hpg + 1]
        for j in range(1, hpg):
            out = jnp.where(head_of_lane == j, a[:, g * hpg + j:g * hpg + j + 1], out)
        return jnp.broadcast_to(out, (a.shape[0], gw))

    def one_sequence(bi):
        xbc = jnp.concatenate([x_ref[bi], bc_ref[bi]], axis=1).astype(F32)
        xbc = _silu(_causal_conv(xbc, tail_ref.at[bi], cw_ref[...]) + cb_ref[...])

        dt = _softplus(dt_ref[bi] + par_ref[1:2, :])
        acum = _cumsum_rows(dt * a)
        acum_t = jnp.concatenate([acum, jnp.zeros((LANES - c, LANES), F32)], axis=0).T \
            if c < LANES else acum.T
        ea = jnp.exp(acum)
        alast = acum[c - 1:c, :]
        ealast = jnp.exp(alast)
        wdec = jnp.exp(alast - acum)

        bms = [xbc[:, d_inner + g * gsz:d_inner + (g + 1) * gsz] for g in groups]
        cms = [xbc[:, d_inner + (SSD_GROUPS + g) * gsz:d_inner + (SSD_GROUPS + g + 1) * gsz]
               for g in groups]
        xgs = [xbc[:, g * gw:(g + 1) * gw] for g in groups]
        cbs = [_dot_nt(cms[g], bms[g]) for g in groups]
        xdts = [xgs[g] * expand(dt, g) for g in groups]
        sts = [st_ref[bi * SSD_GROUPS + g] for g in groups]

        ys = []
        for g in groups:
            lhs = jnp.concatenate(
                [cbs[g] * _decay_matrix(acum[:, h:h + 1], acum_t[h:h + 1, :c], incl)
                 for h in range(g * hpg, (g + 1) * hpg)], axis=1)
            xb = xdts[g].astype(BF16)
            rhs = jnp.concatenate([jnp.where(head_of_lane == j, xb, jnp.zeros_like(xb))
                                   for j in range(hpg)], axis=0)
            y = _dot(lhs, rhs) + _dot(cms[g], sts[g]) * expand(ea, g) + expand(dskip, g) * xgs[g]
            ys.append(y)
        for g in groups:
            st_ref[bi * SSD_GROUPS + g] = sts[g] * expand(ealast, g) + _dot_tn(
                bms[g], xdts[g] * expand(wdec, g))
        for g in groups:
            sl = slice(g * gw, (g + 1) * gw)
            y = ys[g] * _silu(z_ref[bi, :, sl].astype(F32))
            ms = jnp.mean(y * y, axis=-1, keepdims=True)
            o_ref[bi, :, sl] = (y * lax.rsqrt(ms + NORM_EPS) * ng_ref[:, sl]).astype(BF16)

    for bi in range(x_ref.shape[0]):
        one_sequence(bi)


def _ssd_scan(main, dtraw, conv_w, conv_b, a_log, dt_bias, d_skip, norm_g, heads, d_inner):
    b, s, _ = main.shape
    c = min(SSD_CHUNK, s)
    nbc = 2 * SSD_GROUPS * SSD_STATE
    assert nbc == d_inner
    nconv = d_inner + nbc
    padl = lambda v: jnp.concatenate([v, jnp.zeros((LANES - heads,), F32)])
    par = jnp.stack([padl(a_log), padl(dt_bias), padl(d_skip)])
    nb = 1
    tile = lambda j: pl.BlockSpec((nb, c, d_inner), lambda i, t: (i, t, j))
    return pl.pallas_call(
        functools.partial(_ssd_kernel, heads=heads),
        grid=(b // nb, s // c),
        in_specs=[tile(0), tile(1), tile(2),
                  pl.BlockSpec((nb, c, LANES), lambda i, t: (i, t, 0)),
                  _const_spec((CONV_K, nconv)), _const_spec((1, nconv)), _const_spec((3, LANES)),
                  _const_spec((1, d_inner))],
        out_specs=tile(0),
        out_shape=jax.ShapeDtypeStruct((b, s, d_inner), BF16),
        scratch_shapes=[pltpu.VMEM((nb, SUBLANES, nconv), F32),
                        pltpu.VMEM((nb * SSD_GROUPS, SSD_STATE, d_inner // SSD_GROUPS), F32)],
        compiler_params=_params(("arbitrary", "arbitrary")),
        name="ssd_scan",
    )(main, main, main, dtraw, conv_w, conv_b.reshape(1, nconv), par, norm_g.reshape(1, d_inner))


def _deinterleave_heads(w, heads, dk):
    d = w.shape[0]
    return w.reshape(d, heads, dk // 2, 2).transpose(0, 1, 3, 2).reshape(d, heads * dk)


def _pad_gate_cols(w):
    d, n = w.shape
    return jnp.concatenate([w, jnp.zeros((d, LANES - n), w.dtype)], axis=1)


def kernel(x, c, ada_w, ada_b, norm_mix_g, norm_mlp_g, mlp_w1, mlp_w2, final_norm_g, ret_w_in, ret_w_out, gdn_w_in, gdn_conv_w, gdn_A_log, gdn_dt_bias, gdn_norm_g, gdn_w_out, ssd_w_in, ssd_conv_w, ssd_conv_b, ssd_A_log, ssd_dt_bias, ssd_D, ssd_norm_g, ssd_w_out):
    depth = ada_w.shape[0]
    b, s, d = x.shape
    mod = _modulation(c, ada_w, ada_b)
    rope_cos, rope_sin = _rope_table(s, d // RET_HEADS // 2)

    for l in range(depth):
        sh1, sc1, gt1, sh2, sc2, gt2 = [mod[l, :, None, i * d:(i + 1) * d] for i in range(6)]
        kind = l % N_MIXERS
        j = l // N_MIXERS
        if kind == 0:
            w_in = ret_w_in[j]
            dk = d // RET_HEADS
            dv = 2 * d // RET_HEADS
            nqk = RET_HEADS * dk
            w_in = jnp.concatenate([_deinterleave_heads(w_in[:, :nqk], RET_HEADS, dk),
                                    _deinterleave_heads(w_in[:, nqk:2 * nqk], RET_HEADS, dk),
                                    w_in[:, 2 * nqk:]], axis=1).astype(BF16)
            proj, _ = _inproj(x, sh1, sc1, norm_mix_g[l], w_in)
            o = _retention_scan(proj, rope_cos, rope_sin, RET_HEADS, dk, dv)
            w_out = ret_w_out[j]
        elif kind == 1:
            w_in = gdn_w_in[j]
            nmain = 2 * GDN_QK_HEADS * GDN_DK + 2 * GDN_V_HEADS * GDN_DV
            proj, gates = _inproj(x, sh1, sc1, norm_mix_g[l], w_in[:, :nmain].astype(BF16),
                                  _pad_gate_cols(w_in[:, nmain:]).astype(BF16))
            o = _gdn_scan(proj, gates, gdn_conv_w[j], gdn_A_log[j], gdn_dt_bias[j], gdn_norm_g[j])
            w_out = gdn_w_out[j]
        else:
            w_in = ssd_w_in[j]
            d_inner = ssd_w_out.shape[1]
            heads = ssd_A_log.shape[1]
            nmain = 2 * d_inner + 2 * SSD_GROUPS * SSD_STATE
            proj, dtraw = _inproj(x, sh1, sc1, norm_mix_g[l], w_in[:, :nmain].astype(BF16),
                                  _pad_gate_cols(w_in[:, nmain:]).astype(BF16))
            o = _ssd_scan(proj, dtraw, ssd_conv_w[j], ssd_conv_b[j], ssd_A_log[j], ssd_dt_bias[j],
                          ssd_D[j], ssd_norm_g[j], heads, d_inner)
            w_out = ssd_w_out[j]
        x = _outmlp(o, x, gt1, sh2, sc2, gt2, norm_mlp_g[l], w_out.astype(BF16),
                    mlp_w1[l].astype(BF16), mlp_w2[l].astype(BF16),
                    gf=final_norm_g if l == depth - 1 else None)
    return x
```

```python
import functools
import math

import numpy as np
import jax
import jax.numpy as jnp
from jax import lax
from jax.experimental import pallas as pl
from jax.experimental.pallas import tpu as pltpu

F32 = jnp.float32
BF16 = jnp.bfloat16

NORM_EPS = 1e-6
CONV_K = 4
N_MIXERS = 3

RET_HEADS = 4
RET_ROPE_BASE = 10000.0
RET_CHUNK = 256
RET_SEQS_PER_STEP = 2

GDN_QK_HEADS = 8
GDN_V_HEADS = 16
GDN_DK = 128
GDN_DV = 128
GDN_CHUNK = 64
GDN_SOLVE_BLOCK = 16
SEQS_PER_STEP = 4

SSD_HEAD_DIM = 64
SSD_GROUPS = 8
SSD_STATE = 128
SSD_CHUNK = 128

LANES = 128
SUBLANES = 8
VMEM_LIMIT_BYTES = 56 * 1024 * 1024

NT_DIMS = (((1,), (1,)), ((), ()))
TN_DIMS = (((0,), (0,)), ((), ()))


def _dot(a, b):
    return jnp.dot(a.astype(BF16), b.astype(BF16), preferred_element_type=F32)


def _dot_nt(a, b):
    return lax.dot_general(a.astype(BF16), b.astype(BF16), NT_DIMS, preferred_element_type=F32)


def _dot_tn(a, b):
    return lax.dot_general(a.astype(BF16), b.astype(BF16), TN_DIMS, preferred_element_type=F32)


def _silu(x):
    return x * jax.nn.sigmoid(x)


def _softplus(x):
    return jnp.maximum(x, 0.0) + jnp.log1p(jnp.exp(-jnp.abs(x)))


def _params(sem):
    return pltpu.CompilerParams(dimension_semantics=sem, vmem_limit_bytes=VMEM_LIMIT_BYTES)


def _const_spec(shape):
    nd = len(shape)
    return pl.BlockSpec(shape, lambda *_: (0,) * nd, pipeline_mode=pl.Buffered(1))


def _mod_kernel(c_ref, w_ref, b_ref, o_ref):
    cond = _silu(c_ref[...])
    o_ref[0] = jnp.dot(cond, w_ref[0], precision=lax.Precision.HIGHEST,
                       preferred_element_type=F32) + b_ref[0]


def _modulation(c, ada_w, ada_b):
    depth, d, d6 = ada_w.shape
    b = c.shape[0]
    return pl.pallas_call(
        _mod_kernel,
        grid=(depth, d6 // d),
        in_specs=[pl.BlockSpec((b, d), lambda l, j: (0, 0)),
                  pl.BlockSpec((1, d, d), lambda l, j: (l, 0, j)),
                  pl.BlockSpec((1, 1, d), lambda l, j: (l, 0, j))],
        out_specs=pl.BlockSpec((1, b, d), lambda l, j: (l, 0, j)),
        out_shape=jax.ShapeDtypeStruct((depth, b, d6), F32),
        compiler_params=_params(("arbitrary", "arbitrary")),
        name="adaln_mod",
    )(c, ada_w, ada_b.reshape(depth, 1, d6))


def _norm_mod(x, gamma, scale, shift):
    ms = jnp.mean(x * x, axis=-1, keepdims=True)
    return x * lax.rsqrt(ms + NORM_EPS) * gamma * (1.0 + scale) + shift


def _inproj_kernel(x_ref, sh_ref, sc_ref, g_ref, w_ref, *rest, n_chunk, has_gate):
    if has_gate:
        wg_ref, o_ref, og_ref = rest
    else:
        (o_ref,) = rest
    h = _norm_mod(x_ref[0], g_ref[...], sc_ref[0], sh_ref[0]).astype(BF16)
    n = w_ref.shape[1]
    for j in range(n // n_chunk):
        sl = slice(j * n_chunk, (j + 1) * n_chunk)
        o_ref[0, :, sl] = jnp.dot(h, w_ref[:, sl], preferred_element_type=F32).astype(BF16)
    if has_gate:
        og_ref[0] = jnp.dot(h, wg_ref[...], preferred_element_type=F32)


def _inproj(x, shift, scale, gamma, w, wg=None, tm=1024, n_chunk=512):
    b, s, d = x.shape
    n = w.shape[1]
    tm = min(tm, s)
    has_gate = wg is not None
    row = pl.BlockSpec((1, 1, d), lambda i, t: (i, 0, 0))
    in_specs = [pl.BlockSpec((1, tm, d), lambda i, t: (i, t, 0)), row, row,
                _const_spec((1, d)), _const_spec((d, n))]
    args = [x, shift, scale, gamma.reshape(1, d), w]
    out_specs = [pl.BlockSpec((1, tm, n), lambda i, t: (i, t, 0))]
    out_shape = [jax.ShapeDtypeStruct((b, s, n), BF16)]
    if has_gate:
        in_specs.append(_const_spec((d, LANES)))
        args.append(wg)
        out_specs.append(pl.BlockSpec((1, tm, LANES), lambda i, t: (i, t, 0)))
        out_shape.append(jax.ShapeDtypeStruct((b, s, LANES), F32))
    out = pl.pallas_call(
        functools.partial(_inproj_kernel, n_chunk=n_chunk, has_gate=has_gate),
        grid=(b, s // tm),
        in_specs=in_specs, out_specs=out_specs, out_shape=out_shape,
        compiler_params=_params(("arbitrary", "arbitrary")),
        name="norm_inproj",
    )(*args)
    return out if has_gate else (out[0], None)


def _outmlp_kernel(o_ref, x_ref, gt1_ref, sh2_ref, sc2_ref, gt2_ref, g2_ref, wo_ref, w1_ref,
                   w2_ref, *rest, ff_chunk, final):
    if final:
        gf_ref, y_ref = rest
    else:
        (y_ref,) = rest
    y = jnp.dot(o_ref[0], wo_ref[...], preferred_element_type=F32)
    x1 = x_ref[0] + (1.0 + gt1_ref[0]) * y
    h2 = _norm_mod(x1, g2_ref[...], sc2_ref[0], sh2_ref[0]).astype(BF16)
    d_ff = w1_ref.shape[1]
    m = jnp.zeros_like(x1)
    for j in range(d_ff // ff_chunk):
        sl = slice(j * ff_chunk, (j + 1) * ff_chunk)
        u = jnp.maximum(jnp.dot(h2, w1_ref[:, sl], preferred_element_type=F32), 0.0)
        m = m + jnp.dot((u * u).astype(BF16), w2_ref[sl, :], preferred_element_type=F32)
    x2 = x1 + (1.0 + gt2_ref[0]) * m
    if final:
        ms = jnp.mean(x2 * x2, axis=-1, keepdims=True)
        x2 = x2 * lax.rsqrt(ms + NORM_EPS) * gf_ref[...]
    y_ref[0] = x2


def _outmlp(o, x, gt1, sh2, sc2, gt2, g2, w_out, w1, w2, gf=None, tm=512, ff_chunk=1024):
    b, s, d = x.shape
    dv = o.shape[-1]
    d_ff = w1.shape[1]
    tm = min(tm, s)
    final = gf is not None
    row = pl.BlockSpec((1, 1, d), lambda i, t: (i, 0, 0))
    tile = pl.BlockSpec((1, tm, d), lambda i, t: (i, t, 0))
    in_specs = [pl.BlockSpec((1, tm, dv), lambda i, t: (i, t, 0)), tile, row, row, row, row,
                _const_spec((1, d)), _const_spec((dv, d)), _const_spec((d, d_ff)),
                _const_spec((d_ff, d))]
    args = [o, x, gt1, sh2, sc2, gt2, g2.reshape(1, d), w_out, w1, w2]
    if final:
        in_specs.append(_const_spec((1, d)))
        args.append(gf.reshape(1, d))
    return pl.pallas_call(
        functools.partial(_outmlp_kernel, ff_chunk=ff_chunk, final=final),
        grid=(b, s // tm),
        in_specs=in_specs, out_specs=tile,
        out_shape=jax.ShapeDtypeStruct((b, s, d), F32),
        compiler_params=_params(("arbitrary", "arbitrary")),
        name="outproj_mlp",
    )(*args)


def _causal_conv(x, tail_ref, w):
    c = x.shape[0]
    xe = jnp.concatenate([tail_ref[...], x], axis=0)
    tail_ref[...] = x[c - SUBLANES:, :]
    acc = x * w[CONV_K - 1:CONV_K, :]
    for k in range(1, CONV_K):
        acc = acc + pltpu.roll(xe, k, 0)[SUBLANES:, :] * w[CONV_K - 1 - k:CONV_K - k, :]
    return acc


def _cumsum_rows(x):
    c = x.shape[0]
    row = lax.broadcasted_iota(jnp.int32, x.shape, 0)
    sh = 1
    while sh < c:
        x = x + jnp.where(row >= sh, pltpu.roll(x, sh, 0), 0.0)
        sh *= 2
    return x


def _tri_masks(c):
    t = lax.broadcasted_iota(jnp.int32, (c, c), 0)
    s = lax.broadcasted_iota(jnp.int32, (c, c), 1)
    return t >= s, t > s


def _decay_matrix(col, rowv, incl):
    return jnp.where(incl, jnp.exp(jnp.where(incl, col - rowv, 0.0)), 0.0)


def _rope_table_kernel(invf_ref, cos_ref, sin_ref):
    rows = cos_ref.shape[0]
    pos = (pl.program_id(0) * rows + lax.broadcasted_iota(jnp.int32, cos_ref.shape, 0)).astype(F32)
    ang = pos * invf_ref[...]
    cos_ref[...] = jnp.cos(ang)
    sin_ref[...] = jnp.sin(ang)


def _rope_table(s, half, rows=1024):
    rows = min(rows, s)
    inv_freq = RET_ROPE_BASE ** (-jnp.linspace(0.0, 1.0, half, dtype=F32))
    spec = pl.BlockSpec((rows, half), lambda t: (t, 0))
    return pl.pallas_call(
        _rope_table_kernel,
        grid=(s // rows,),
        in_specs=[_const_spec((1, half))],
        out_specs=[spec, spec],
        out_shape=[jax.ShapeDtypeStruct((s, half), F32)] * 2,
        compiler_params=_params(("arbitrary",)),
        name="rope_table",
    )(inv_freq.reshape(1, half))


def _ret_kernel(q_ref, k_ref, v_ref, g_ref, cos_ref, sin_ref, dmask_ref, dvec_ref, o_ref, r_ref, *,
                heads):
    dk = q_ref.shape[2] // heads
    dv = v_ref.shape[2] // heads
    half = dk // 2
    step = pl.program_id(1)

    @pl.when(step == 0)
    def _():
        r_ref[...] = jnp.zeros_like(r_ref)

    cs = cos_ref[...]
    sn = sin_ref[...]
    dvec = dvec_ref[...]

    nb = q_ref.shape[0]

    def rope(ref, bi, h):
        a = ref[bi, :, h * dk:h * dk + half].astype(F32)
        b = ref[bi, :, h * dk + half:(h + 1) * dk].astype(F32)
        return jnp.concatenate([a * cs - b * sn, a * sn + b * cs], axis=1)

    units = [(bi, h) for bi in range(nb) for h in range(heads)]
    qfs = [rope(q_ref, bi, h) for bi, h in units]
    khs = [rope(k_ref, bi, h) * (dk ** -0.5) for bi, h in units]
    vhs = [v_ref[bi, :, h * dv:(h + 1) * dv] for bi, h in units]
    scs = [_dot_nt(qfs[u], khs[u]) * dmask_ref[h] for u, (bi, h) in enumerate(units)]
    rs = [r_ref[bi * heads + h] for bi, h in units]
    os_ = [_dot(jnp.concatenate([scs[u].astype(BF16), (qfs[u] * dvec[:, h:h + 1]).astype(BF16)],
                                axis=1),
                jnp.concatenate([vhs[u], rs[u].astype(BF16)], axis=0))
           for u, (bi, h) in enumerate(units)]
    for u, (bi, h) in enumerate(units):
        zeta = dvec[:, heads + h:heads + h + 1]
        dc = dvec[0:1, 2 * heads + h:2 * heads + h + 1]
        r_ref[bi * heads + h] = rs[u] * dc + _dot_tn(khs[u] * zeta, vhs[u])
    for u, (bi, h) in enumerate(units):
        o = os_[u]
        ms = jnp.mean(o * o, axis=-1, keepdims=True)
        on = o * lax.rsqrt(ms + NORM_EPS)
        gate = _silu(g_ref[bi, :, h * dv:(h + 1) * dv].astype(F32))
        o_ref[bi, :, h * dv:(h + 1) * dv] = (gate * on).astype(BF16)


def _retention_scan(qkvg, rope_cos, rope_sin, heads, dk, dv):
    b, s, _ = qkvg.shape
    c = min(RET_CHUNK, s)
    nqk = heads * dk
    nv = heads * dv
    lg = jnp.log1p(-jnp.exp2(-5.0 - jnp.arange(heads, dtype=F32)))
    pos = jnp.arange(c, dtype=F32)
    rel = pos[:, None] - pos[None, :]
    causal = rel >= 0
    dmask = jnp.where(causal, jnp.exp(jnp.where(causal, rel, 0.0) * lg[:, None, None]), 0.0)
    xi = jnp.exp((pos[:, None] + 1.0) * lg[None, :])
    zeta = jnp.exp((c - 1.0 - pos)[:, None] * lg[None, :])
    dcs = jnp.broadcast_to(jnp.exp(c * lg)[None, :], (c, heads))
    dvec = jnp.concatenate([xi, zeta, dcs, jnp.zeros((c, LANES - 3 * heads), F32)], axis=1)
    assert nv == 2 * nqk
    rope_spec = pl.BlockSpec((c, dk // 2), lambda i, t: (t, 0))
    nb = RET_SEQS_PER_STEP if b % RET_SEQS_PER_STEP == 0 else 1
    return pl.pallas_call(
        functools.partial(_ret_kernel, heads=heads),
        grid=(b // nb, s // c),
        in_specs=[pl.BlockSpec((nb, c, nqk), lambda i, t: (i, t, 0)),
                  pl.BlockSpec((nb, c, nqk), lambda i, t: (i, t, 1)),
                  pl.BlockSpec((nb, c, nv), lambda i, t: (i, t, 1)),
                  pl.BlockSpec((nb, c, nv), lambda i, t: (i, t, 2)),
                  rope_spec, rope_spec, _const_spec((heads, c, c)), _const_spec((c, LANES))],
        out_specs=pl.BlockSpec((nb, c, nv), lambda i, t: (i, t, 0)),
        out_shape=jax.ShapeDtypeStruct((b, s, nv), BF16),
        scratch_shapes=[pltpu.VMEM((nb * heads, dk, dv), F32)],
        compiler_params=_params(("arbitrary", "arbitrary")),
        name="retention_scan",
    )(qkvg, qkvg, qkvg, qkvg, rope_cos, rope_sin, dmask, dvec)


def _unit_lower_solve(amats, rhs):
    c = amats[0].shape[0]
    blk = GDN_SOLVE_BLOCK
    row = lax.broadcasted_iota(jnp.int32, (c, c), 0)
    colm = lax.broadcasted_iota(jnp.int32, (c, c), 1)
    same = (row // blk) == (colm // blk)
    eye = (row == colm).astype(F32)
    ps = [jnp.where(same, -a, 0.0) for a in amats]
    offs = [jnp.where(same, 0.0, a) for a in amats]
    ts = [eye + p for p in ps]
    ps = [_dot(p, p) for p in ps]
    n_sq = int(math.log2(blk))
    for it in range(1, n_sq):
        if it + 1 < n_sq:
            tps = [_dot(jnp.concatenate([t, p], axis=0), p) for t, p in zip(ts, ps)]
            ts = [t + tp[:c] for t, tp in zip(ts, tps)]
            ps = [tp[c:] for tp in tps]
        else:
            ts = [t + _dot(t, p) for t, p in zip(ts, ps)]
    nps = [_dot(t, off) for t, off in zip(ts, offs)]
    xs = [_dot(t, r) for t, r in zip(ts, rhs)]
    for i in range(1, c // blk):
        lo, hi = i * blk, (i + 1) * blk
        xs = [jnp.concatenate([x[:lo], x[lo:hi] - _dot(n[lo:hi, :], x)] + ([x[hi:]] if hi < c else []),
                              axis=0) for n, x in zip(nps, xs)]
    return xs

def _gdn_kernel(qkv_ref, z_ref, gate_ref, cw_ref, par_ref, ng_ref, o_ref, tail_ref, st_ref):
    c = qkv_ref.shape[1]
    nqk = GDN_QK_HEADS * GDN_DK
    rep = GDN_V_HEADS // GDN_QK_HEADS
    step = pl.program_id(1)

    @pl.when(step == 0)
    def _():
        tail_ref[...] = jnp.zeros_like(tail_ref)
        st_ref[...] = jnp.zeros_like(st_ref)

    incl, strict = _tri_masks(c)
    ng = ng_ref[...]
    heads = range(GDN_V_HEADS)
    col = lambda a, h: a[:, GDN_V_HEADS + h:GDN_V_HEADS + h + 1]

    def one_sequence(bi):
        qkv = _silu(_causal_conv(qkv_ref[bi].astype(F32), tail_ref.at[bi], cw_ref[...]))

        gates = gate_ref[bi]
        beta = jax.nn.sigmoid(gates)
        g = -jnp.exp(par_ref[0:1, :]) * _softplus(gates + par_ref[1:2, :])
        gc = _cumsum_rows(g)
        gct = jnp.concatenate([gc, jnp.zeros((LANES - c, LANES), F32)], axis=0).T
        eg = jnp.exp(gc)
        glast = gc[c - 1:c, :]
        eglast = jnp.exp(glast)
        kdec = jnp.exp(glast - gc)

        qs, ks, qkk = [], [], []
        for j in range(GDN_QK_HEADS):
            q = qkv[:, j * GDN_DK:(j + 1) * GDN_DK]
            k = qkv[:, nqk + j * GDN_DK:nqk + (j + 1) * GDN_DK]
            q = q * lax.rsqrt(jnp.sum(q * q, axis=-1, keepdims=True) + 1e-6) * (GDN_DK ** -0.5)
            k = k * lax.rsqrt(jnp.sum(k * k, axis=-1, keepdims=True) + 1e-6)
            qs.append(q)
            ks.append(k)
            qkk.append(_dot_nt(jnp.concatenate([q, k], axis=0), k))

        decays, amats, xs = [], [], []
        for h in heads:
            j = h // rep
            v = qkv[:, 2 * nqk + h * GDN_DV:2 * nqk + (h + 1) * GDN_DV]
            bcol = beta[:, h:h + 1]
            grow = gct[GDN_V_HEADS + h:GDN_V_HEADS + h + 1, :c]
            decay = _decay_matrix(col(gc, h), grow, incl)
            decays.append(decay)
            amats.append(jnp.where(strict, bcol * qkk[j][c:, :] * decay, 0.0))
            xs.append(jnp.concatenate([v * bcol, ks[j] * (bcol * col(eg, h))], axis=1))
        xs = _unit_lower_solve(amats, xs)

        sts = [st_ref[bi * GDN_V_HEADS + h] for h in heads]
        v_news = [xs[h][:, :GDN_DV] - _dot(xs[h][:, GDN_DV:], sts[h]) for h in heads]
        outs = []
        for h in heads:
            j = h // rep
            lhs = jnp.concatenate([qs[j] * col(eg, h), qkk[j][:c, :] * decays[h]], axis=1)
            outs.append(_dot(lhs, jnp.concatenate([sts[h], v_news[h]], axis=0)))
        for h in heads:
            st_ref[bi * GDN_V_HEADS + h] = sts[h] * col(eglast, h) + _dot_tn(
                ks[h // rep] * col(kdec, h), v_news[h])
        for h in heads:
            o = outs[h]
            ms = jnp.mean(o * o, axis=-1, keepdims=True)
            on = o * lax.rsqrt(ms + NORM_EPS) * ng
            zg = _silu(z_ref[bi, :, h * GDN_DV:(h + 1) * GDN_DV].astype(F32))
            o_ref[bi, :, h * GDN_DV:(h + 1) * GDN_DV] = (on * zg).astype(BF16)

    for bi in range(qkv_ref.shape[0]):
        one_sequence(bi)


def _gdn_scan(main, gates, conv_w, a_log, dt_bias, norm_g):
    b, s, _ = main.shape
    c = GDN_CHUNK
    nconv = 2 * GDN_QK_HEADS * GDN_DK + GDN_V_HEADS * GDN_DV
    nv = GDN_V_HEADS * GDN_DV
    assert nconv == 2 * nv
    pad = jnp.zeros((GDN_V_HEADS,), F32)
    tail = jnp.zeros((LANES - 2 * GDN_V_HEADS,), F32)
    par = jnp.stack([jnp.concatenate([pad, a_log, tail]), jnp.concatenate([pad, dt_bias, tail])])
    nb = SEQS_PER_STEP if b % SEQS_PER_STEP == 0 else 1
    return pl.pallas_call(
        _gdn_kernel,
        grid=(b // nb, s // c),
        in_specs=[pl.BlockSpec((nb, c, nconv), lambda i, t: (i, t, 0)),
                  pl.BlockSpec((nb, c, nv), lambda i, t: (i, t, 2)),
                  pl.BlockSpec((nb, c, LANES), lambda i, t: (i, t, 0)),
                  _const_spec((CONV_K, nconv)), _const_spec((2, LANES)), _const_spec((1, GDN_DV))],
        out_specs=pl.BlockSpec((nb, c, nv), lambda i, t: (i, t, 0)),
        out_shape=jax.ShapeDtypeStruct((b, s, nv), BF16),
        scratch_shapes=[pltpu.VMEM((nb, SUBLANES, nconv), F32),
                        pltpu.VMEM((nb * GDN_V_HEADS, GDN_DK, GDN_DV), F32)],
        compiler_params=_params(("arbitrary", "arbitrary")),
        name="gdn_scan",
    )(main, main, gates, conv_w, par, norm_g.reshape(1, GDN_DV))


def _ssd_kernel(z_ref, x_ref, bc_ref, dt_ref, cw_ref, cb_ref, par_ref, ng_ref, o_ref, tail_ref,
                st_ref, *, heads):
    c = x_ref.shape[1]
    d_inner = x_ref.shape[2]
    p_dim = d_inner // heads
    gsz = SSD_STATE
    hpg = heads // SSD_GROUPS
    step = pl.program_id(1)

    @pl.when(step == 0)
    def _():
        tail_ref[...] = jnp.zeros_like(tail_ref)
        st_ref[...] = jnp.zeros_like(st_ref)

    a = -jnp.exp(par_ref[0:1, :])
    dskip = par_ref[2:3, :]
    incl, _ = _tri_masks(c)
    gw = hpg * p_dim
    head_of_lane = lax.broadcasted_iota(jnp.int32, (1, gw), 1) // p_dim
    groups = range(SSD_GROUPS)

    def expand(a, g):
        out = a[:, g * hpg:g * hpg + 1]
        for j in range(1, hpg):
            out = jnp.where(head_of_lane == j, a[:, g * hpg + j:g * hpg + j + 1], out)
        return jnp.broadcast_to(out, (a.shape[0], gw))

    def one_sequence(bi):
        xbc = jnp.concatenate([x_ref[bi], bc_ref[bi]], axis=1).astype(F32)
        xbc = _silu(_causal_conv(xbc, tail_ref.at[bi], cw_ref[...]) + cb_ref[...])

        dt = _softplus(dt_ref[bi] + par_ref[1:2, :])
        acum = _cumsum_rows(dt * a)
        acum_t = jnp.concatenate([acum, jnp.zeros((LANES - c, LANES), F32)], axis=0).T \
            if c < LANES else acum.T
        ea = jnp.exp(acum)
        alast = acum[c - 1:c, :]
        ealast = jnp.exp(alast)
        wdec = jnp.exp(alast - acum)

        bms = [xbc[:, d_inner + g * gsz:d_inner + (g + 1) * gsz] for g in groups]
        cms = [xbc[:, d_inner + (SSD_GROUPS + g) * gsz:d_inner + (SSD_GROUPS + g + 1) * gsz]
               for g in groups]
        xgs = [xbc[:, g * gw:(g + 1) * gw] for g in groups]
        cbs = [_dot_nt(cms[g], bms[g]) for g in groups]
        xdts = [xgs[g] * expand(dt, g) for g in groups]
        sts = [st_ref[bi * SSD_GROUPS + g] for g in groups]

        ys = []
        for g in groups:
            lhs = jnp.concatenate(
                [cbs[g] * _decay_matrix(acum[:, h:h + 1], acum_t[h:h + 1, :c], incl)
                 for h in range(g * hpg, (g + 1) * hpg)], axis=1)
            xb = xdts[g].astype(BF16)
            rhs = jnp.concatenate([jnp.where(head_of_lane == j, xb, jnp.zeros_like(xb))
                                   for j in range(hpg)], axis=0)
            y = _dot(lhs, rhs) + _dot(cms[g], sts[g]) * expand(ea, g) + expand(dskip, g) * xgs[g]
            ys.append(y)
        for g in groups:
            st_ref[bi * SSD_GROUPS + g] = sts[g] * expand(ealast, g) + _dot_tn(
                bms[g], xdts[g] * expand(wdec, g))
        for g in groups:
            sl = slice(g * gw, (g + 1) * gw)
            y = ys[g] * _silu(z_ref[bi, :, sl].astype(F32))
            ms = jnp.mean(y * y, axis=-1, keepdims=True)
            o_ref[bi, :, sl] = (y * lax.rsqrt(ms + NORM_EPS) * ng_ref[:, sl]).astype(BF16)

    for bi in range(x_ref.shape[0]):
        one_sequence(bi)


def _ssd_scan(main, dtraw, conv_w, conv_b, a_log, dt_bias, d_skip, norm_g, heads, d_inner):
    b, s, _ = main.shape
    c = min(SSD_CHUNK, s)
    nbc = 2 * SSD_GROUPS * SSD_STATE
    assert nbc == d_inner
    nconv = d_inner + nbc
    padl = lambda v: jnp.concatenate([v, jnp.zeros((LANES - heads,), F32)])
    par = jnp.stack([padl(a_log), padl(dt_bias), padl(d_skip)])
    nb = 1
    tile = lambda j: pl.BlockSpec((nb, c, d_inner), lambda i, t: (i, t, j))
    return pl.pallas_call(
        functools.partial(_ssd_kernel, heads=heads),
        grid=(b // nb, s // c),
        in_specs=[tile(0), tile(1), tile(2),
                  pl.BlockSpec((nb, c, LANES), lambda i, t: (i, t, 0)),
                  _const_spec((CONV_K, nconv)), _const_spec((1, nconv)), _const_spec((3, LANES)),
                  _const_spec((1, d_inner))],
        out_specs=tile(0),
        out_shape=jax.ShapeDtypeStruct((b, s, d_inner), BF16),
        scratch_shapes=[pltpu.VMEM((nb, SUBLANES, nconv), F32),
                        pltpu.VMEM((nb * SSD_GROUPS, SSD_STATE, d_inner // SSD_GROUPS), F32)],
        compiler_params=_params(("arbitrary", "arbitrary")),
        name="ssd_scan",
    )(main, main, main, dtraw, conv_w, conv_b.reshape(1, nconv), par, norm_g.reshape(1, d_inner))


def _deinterleave_heads(w, heads, dk):
    d = w.shape[0]
    return w.reshape(d, heads, dk // 2, 2).transpose(0, 1, 3, 2).reshape(d, heads * dk)


def _pad_gate_cols(w):
    d, n = w.shape
    return jnp.concatenate([w, jnp.zeros((d, LANES - n), w.dtype)], axis=1)


def kernel(x, c, ada_w, ada_b, norm_mix_g, norm_mlp_g, mlp_w1, mlp_w2, final_norm_g, ret_w_in, ret_w_out, gdn_w_in, gdn_conv_w, gdn_A_log, gdn_dt_bias, gdn_norm_g, gdn_w_out, ssd_w_in, ssd_conv_w, ssd_conv_b, ssd_A_log, ssd_dt_bias, ssd_D, ssd_norm_g, ssd_w_out):
    depth = ada_w.shape[0]
    b, s, d = x.shape
    mod = _modulation(c, ada_w, ada_b)
    rope_cos, rope_sin = _rope_table(s, d // RET_HEADS // 2)

    for l in range(depth):
        sh1, sc1, gt1, sh2, sc2, gt2 = [mod[l, :, None, i * d:(i + 1) * d] for i in range(6)]
        kind = l % N_MIXERS
        j = l // N_MIXERS
        if kind == 0:
            w_in = ret_w_in[j]
            dk = d // RET_HEADS
            dv = 2 * d // RET_HEADS
            nqk = RET_HEADS * dk
            w_in = jnp.concatenate([_deinterleave_heads(w_in[:, :nqk], RET_HEADS, dk),
                                    _deinterleave_heads(w_in[:, nqk:2 * nqk], RET_HEADS, dk),
                                    w_in[:, 2 * nqk:]], axis=1).astype(BF16)
            proj, _ = _inproj(x, sh1, sc1, norm_mix_g[l], w_in)
            o = _retention_scan(proj, rope_cos, rope_sin, RET_HEADS, dk, dv)
            w_out = ret_w_out[j]
        elif kind == 1:
            w_in = gdn_w_in[j]
            nmain = 2 * GDN_QK_HEADS * GDN_DK + 2 * GDN_V_HEADS * GDN_DV
            proj, gates = _inproj(x, sh1, sc1, norm_mix_g[l], w_in[:, :nmain].astype(BF16),
                                  _pad_gate_cols(w_in[:, nmain:]).astype(BF16))
            o = _gdn_scan(proj, gates, gdn_conv_w[j], gdn_A_log[j], gdn_dt_bias[j], gdn_norm_g[j])
            w_out = gdn_w_out[j]
        else:
            w_in = ssd_w_in[j]
            d_inner = ssd_w_out.shape[1]
            heads = ssd_A_log.shape[1]
            nmain = 2 * d_inner + 2 * SSD_GROUPS * SSD_STATE
            proj, dtraw = _inproj(x, sh1, sc1, norm_mix_g[l], w_in[:, :nmain].astype(BF16),
                                  _pad_gate_cols(w_in[:, nmain:]).astype(BF16))
            o = _ssd_scan(proj, dtraw, ssd_conv_w[j], ssd_conv_b[j], ssd_A_log[j], ssd_dt_bias[j],
                          ssd_D[j], ssd_norm_g[j], heads, d_inner)
            w_out = ssd_w_out[j]
        x = _outmlp(o, x, gt1, sh2, sc2, gt2, norm_mlp_g[l], w_out.astype(BF16),
                    mlp_w1[l].astype(BF16), mlp_w2[l].astype(BF16),
                    gf=final_norm_g if l == depth - 1 else None)
    return x
```
